```python
import math
import jax, jax.numpy as jnp
from jax import lax
import numpy as np


D_MODEL = 1024
BATCH = 2
SEQ = 8192
DEPTH = 2

D_FF = 2816
DSA_HEADS = 4
DSA_HEAD_DIM = 128
IDX_HEADS = 8
IDX_DIM = 64
TOPK_MAX = 256
RWKV_HEADS = 8
RWKV_HEAD_DIM = 64
RWKV_WIDTH = RWKV_HEADS * RWKV_HEAD_DIM
DECAY_LORA = 64
AAA_LORA = 64
GATE_LORA = 128
RWKV_GN_EPS = 64e-5
DIFF_HEADS = 4
DIFF_QK_DIM = 64
DIFF_V_DIM = 2 * DIFF_QK_DIM
N_BRANCH = 3
BRANCH_WIDTH = 512
Q_BLOCK = 128
NORM_EPS = 1e-6

A_QKV = DSA_HEADS * DSA_HEAD_DIM
A_IQ = IDX_HEADS * IDX_DIM
DSA_SEGMENTS = (A_QKV, A_QKV, A_QKV, A_IQ, IDX_DIM, IDX_HEADS)
RWKV_SEGMENTS = (RWKV_WIDTH, RWKV_WIDTH, RWKV_WIDTH, DECAY_LORA, AAA_LORA, GATE_LORA)
DIFF_SEGMENTS = (DIFF_HEADS * 2 * DIFF_QK_DIM, DIFF_HEADS * 2 * DIFF_QK_DIM, DIFF_HEADS * DIFF_V_DIM)
GATE_COLS = N_BRANCH * D_MODEL
GROUP_SEGMENTS = (sum(DSA_SEGMENTS), sum(RWKV_SEGMENTS), sum(DIFF_SEGMENTS), GATE_COLS)
D_IN = sum(GROUP_SEGMENTS)
RWKV_STREAM = sum(RWKV_SEGMENTS)
GROUP_CUTS = tuple(sum(GROUP_SEGMENTS[:i]) for i in range(1, len(GROUP_SEGMENTS)))
DSA_CUTS = tuple(sum(DSA_SEGMENTS[:i]) for i in range(1, len(DSA_SEGMENTS)))
RWKV_CUTS = tuple(sum(RWKV_SEGMENTS[:i]) for i in range(1, len(RWKV_SEGMENTS)))
DIFF_CUTS = tuple(sum(DIFF_SEGMENTS[:i]) for i in range(1, len(DIFF_SEGMENTS)))

kernel_name = 'hybrid_dsa_rwkv7_diffattn_block'


def _rms_norm(x, g):
    xf = x.astype(jnp.float32)
    y = xf * lax.rsqrt(jnp.mean(xf * xf, axis=-1, keepdims=True) + NORM_EPS)
    return (y * g.astype(jnp.float32)).astype(x.dtype)


def _swiglu(h, w_gate, w_up, w_down):
    return (jax.nn.silu(h @ w_gate) * (h @ w_up)) @ w_down


def _dsa_branch(z):
    B, L, _ = z.shape
    topk = min(TOPK_MAX, L // 4)
    q, k, v, qi, ki, wi = jnp.split(z, DSA_CUTS, axis=-1)
    q = q.reshape(B, L, DSA_HEADS, DSA_HEAD_DIM)
    k = k.reshape(B, L, DSA_HEADS, DSA_HEAD_DIM)
    v = v.reshape(B, L, DSA_HEADS, DSA_HEAD_DIM)
    qi = qi.reshape(B, L, IDX_HEADS, IDX_DIM)
    ki32 = ki.astype(jnp.float32)
    wi32 = wi.astype(jnp.float32) * (IDX_HEADS ** -0.5)
    key_pos = jnp.arange(L)
    gather = jax.vmap(lambda a, ix: a[ix])

    def block(i):
        t0 = i * Q_BLOCK
        qb = lax.dynamic_slice_in_dim(q, t0, Q_BLOCK, axis=1)
        qib = lax.dynamic_slice_in_dim(qi, t0, Q_BLOCK, axis=1).astype(jnp.float32)
        wib = lax.dynamic_slice_in_dim(wi32, t0, Q_BLOCK, axis=1)
        qpos = t0 + jnp.arange(Q_BLOCK)
        causal = key_pos[None, :] <= qpos[:, None]
        dots = jnp.einsum('bqhd,bsd->bqhs', qib, ki32) * (IDX_DIM ** -0.5)
        score = jnp.einsum('bqhs,bqh->bqs', jax.nn.relu(dots), wib)
        score = jnp.where(causal[None], score, -jnp.inf)
        _, sel = lax.top_k(score, topk)
        valid = sel <= qpos[None, :, None]
        ks = gather(k, sel)
        vs = gather(v, sel)
        logits = jnp.einsum('bqhd,bqkhd->bhqk', qb, ks).astype(jnp.float32) * (DSA_HEAD_DIM ** -0.5)
        logits = jnp.where(valid[:, None], logits, -jnp.inf)
        p = jax.nn.softmax(logits, axis=-1).astype(v.dtype)
        return jnp.einsum('bhqk,bqkhd->bqhd', p, vs)

    out = lax.map(block, jnp.arange(L // Q_BLOCK))
    return jnp.moveaxis(out, 0, 1).reshape(B, L, A_QKV)


def _rwkv7_branch(z, mu, w0, w_up, a0, a_up, g_up, k_k, k_a, r_k, ln_w, ln_b):
    B, L, _ = z.shape
    f32 = jnp.float32
    z_prev = jnp.pad(z, ((0, 0), (1, 0), (0, 0)))[:, :L]
    z = z + (z_prev - z) * mu
    r, k, v, wc, ac, gc = jnp.split(z, RWKV_CUTS, axis=-1)
    w_log = -jax.nn.softplus(-(w0 + jnp.tanh(wc) @ w_up)) - 0.5
    decay = jnp.exp(-jnp.exp(w_log.astype(f32)))
    a = jax.nn.sigmoid(a0 + ac @ a_up)
    g = jax.nn.sigmoid(gc) @ g_up
    heads = lambda t: t.reshape(B, L, RWKV_HEADS, RWKV_HEAD_DIM).astype(f32)
    kk = heads(k * k_k)
    kk = kk / jnp.maximum(jnp.linalg.norm(kk, axis=-1, keepdims=True), 1e-12)
    k = k * (1 + (a - 1) * k_a)
    rh, kh, vh, ah, wh = heads(r), heads(k), heads(v), heads(a), heads(decay)

    def step(S, inp):
        r_t, w_t, k_t, v_t, kk_t, b_t = inp
        S = (S * w_t[:, :, None, :]
             + jnp.einsum('bhvk,bhk->bhv', S, -kk_t)[..., None] * b_t[:, :, None, :]
             + v_t[..., None] * k_t[:, :, None, :])
        return S, jnp.einsum('bhvk,bhk->bhv', S, r_t)

    xs = tuple(jnp.moveaxis(t, 1, 0) for t in (rh, wh, kh, vh, kk, kk * ah))
    S0 = jnp.zeros((B, RWKV_HEADS, RWKV_HEAD_DIM, RWKV_HEAD_DIM), f32)
    _, y = lax.scan(step, S0, xs)
    y = jnp.moveaxis(y, 0, 1)
    mean = jnp.mean(y, axis=-1, keepdims=True)
    var = jnp.mean(jnp.square(y - mean), axis=-1, keepdims=True)
    y = ((y - mean) * lax.rsqrt(var + RWKV_GN_EPS)).reshape(B, L, RWKV_WIDTH)
    y = y * ln_w.astype(f32) + ln_b.astype(f32)
    bonus = jnp.sum(rh * kh * r_k.reshape(RWKV_HEADS, RWKV_HEAD_DIM).astype(f32), axis=-1, keepdims=True) * vh
    out = (y + bonus.reshape(B, L, RWKV_WIDTH)) * g.astype(f32)
    return out.astype(z.dtype)


def _diff_branch(z, lq1, lk1, lq2, lk2, subln, lambda_init):
    B, L, _ = z.shape
    q, k, v = jnp.split(z, DIFF_CUTS, axis=-1)
    q = q.reshape(B, L, DIFF_HEADS, 2, DIFF_QK_DIM)
    k = k.reshape(B, L, DIFF_HEADS, 2, DIFF_QK_DIM)
    v = v.reshape(B, L, DIFF_HEADS, DIFF_V_DIM)
    lam = (jnp.exp(jnp.sum(lq1.astype(jnp.float32) * lk1.astype(jnp.float32)))
           - jnp.exp(jnp.sum(lq2.astype(jnp.float32) * lk2.astype(jnp.float32))) + lambda_init)
    key_pos = jnp.arange(L)

    def block(i):
        t0 = i * Q_BLOCK
        qb = lax.dynamic_slice_in_dim(q, t0, Q_BLOCK, axis=1)
        qpos = t0 + jnp.arange(Q_BLOCK)
        causal = key_pos[None, :] <= qpos[:, None]
        s = jnp.einsum('bqhmd,bshmd->bhmqs', qb, k).astype(jnp.float32) * (DIFF_QK_DIM ** -0.5)
        s = jnp.where(causal[None, None, None], s, -jnp.inf)
        p = jax.nn.softmax(s, axis=-1)
        attn = (p[:, :, 0] - lam * p[:, :, 1]).astype(v.dtype)
        return jnp.einsum('bhqs,bshd->bqhd', attn, v)

    out = jnp.moveaxis(lax.map(block, jnp.arange(L // Q_BLOCK)), 0, 1).reshape(B, L, DIFF_HEADS, DIFF_V_DIM)
    out = _rms_norm(out, subln) * (1.0 - lambda_init)
    return out.reshape(B, L, DIFF_HEADS * DIFF_V_DIM)


def _mixer(h, w_in, rwkv_params, diff_params, w_branch, w_out, lambda_init):
    B, L, _ = h.shape
    z = h @ w_in
    z_dsa, z_rwkv, z_diff, z_gate = jnp.split(z, GROUP_CUTS, axis=-1)
    y_a = _dsa_branch(z_dsa)
    y_b = _rwkv7_branch(z_rwkv, *rwkv_params)
    y_c = _diff_branch(z_diff, *diff_params, lambda_init)
    branches = jnp.stack([y_a, y_b, y_c], axis=2)
    proj = jnp.einsum('blnc,ncd->blnd', branches, w_branch)
    gates = jax.nn.sigmoid(z_gate.reshape(B, L, N_BRANCH, D_MODEL))
    merged = jnp.sum(gates * proj, axis=2)
    return merged @ w_out


def setup_inputs(seed: int = 0) -> dict:
    key = jax.random.key(seed)
    ks = jax.random.split(key, 32)
    f32 = jnp.float32
    nrm = lambda k, shape, s: jax.random.normal(k, shape, f32) * s
    gain = lambda k, shape: 1.0 + 0.05 * jax.random.normal(k, shape, f32)
    Ld = DEPTH
    return {
        'x': jax.random.normal(ks[0], (BATCH, SEQ, D_MODEL), f32),
        'ffn1_norm_pre': gain(ks[1], (Ld, D_MODEL)),
        'ffn1_norm_post': gain(ks[2], (Ld, D_MODEL)),
        'ffn1_w_gate': nrm(ks[3], (Ld, D_MODEL, D_FF), D_MODEL ** -0.5),
        'ffn1_w_up': nrm(ks[4], (Ld, D_MODEL, D_FF), D_MODEL ** -0.5),
        'ffn1_w_down': nrm(ks[5], (Ld, D_FF, D_MODEL), D_FF ** -0.5),
        'mix_norm_pre': gain(ks[6], (Ld, D_MODEL)),
        'mix_norm_post': gain(ks[7], (Ld, D_MODEL)),
        'w_in': nrm(ks[8], (Ld, D_MODEL, D_IN), D_MODEL ** -0.5),
        'rwkv_mu': jax.random.uniform(ks[9], (Ld, RWKV_STREAM), f32),
        'rwkv_w0': jax.random.uniform(ks[10], (Ld, RWKV_WIDTH), f32, -6.0, -1.0),
        'rwkv_w_up': nrm(ks[11], (Ld, DECAY_LORA, RWKV_WIDTH), 0.1),
        'rwkv_a0': nrm(ks[12], (Ld, RWKV_WIDTH), 0.1),
        'rwkv_a_up': nrm(ks[13], (Ld, AAA_LORA, RWKV_WIDTH), 0.5 * AAA_LORA ** -0.5),
        'rwkv_g_up': nrm(ks[14], (Ld, GATE_LORA, RWKV_WIDTH), GATE_LORA ** -0.5),
        'rwkv_k_k': 0.85 + 0.05 * jax.random.normal(ks[15], (Ld, RWKV_WIDTH), f32),
        'rwkv_k_a': gain(ks[16], (Ld, RWKV_WIDTH)),
        'rwkv_r_k': nrm(ks[17], (Ld, RWKV_WIDTH), 0.1),
        'rwkv_ln_w': gain(ks[18], (Ld, RWKV_WIDTH)),
        'rwkv_ln_b': nrm(ks[19], (Ld, RWKV_WIDTH), 0.01),
        'diff_lambda_q1': nrm(ks[20], (Ld, DIFF_QK_DIM), 0.1),
        'diff_lambda_k1': nrm(ks[21], (Ld, DIFF_QK_DIM), 0.1),
        'diff_lambda_q2': nrm(ks[22], (Ld, DIFF_QK_DIM), 0.1),
        'diff_lambda_k2': nrm(ks[23], (Ld, DIFF_QK_DIM), 0.1),
        'diff_subln': gain(ks[24], (Ld, DIFF_V_DIM)),
        'w_branch': nrm(ks[25], (Ld, N_BRANCH, BRANCH_WIDTH, D_MODEL), BRANCH_WIDTH ** -0.5),
        'w_out': nrm(ks[26], (Ld, D_MODEL, D_MODEL), D_MODEL ** -0.5),
        'ffn2_norm_pre': gain(ks[27], (Ld, D_MODEL)),
        'ffn2_norm_post': gain(ks[28], (Ld, D_MODEL)),
        'ffn2_w_gate': nrm(ks[29], (Ld, D_MODEL, D_FF), D_MODEL ** -0.5),
        'ffn2_w_up': nrm(ks[30], (Ld, D_MODEL, D_FF), D_MODEL ** -0.5),
        'ffn2_w_down': nrm(ks[31], (Ld, D_FF, D_MODEL), D_FF ** -0.5),
    }


def reference(x, ffn1_norm_pre, ffn1_norm_post, ffn1_w_gate, ffn1_w_up, ffn1_w_down,
              mix_norm_pre, mix_norm_post, w_in, rwkv_mu, rwkv_w0, rwkv_w_up, rwkv_a0, rwkv_a_up,
              rwkv_g_up, rwkv_k_k, rwkv_k_a, rwkv_r_k, rwkv_ln_w, rwkv_ln_b,
              diff_lambda_q1, diff_lambda_k1, diff_lambda_q2, diff_lambda_k2, diff_subln,
              w_branch, w_out, ffn2_norm_pre, ffn2_norm_post, ffn2_w_gate, ffn2_w_up, ffn2_w_down):
    for l in range(DEPTH):
        h = _rms_norm(x, ffn1_norm_pre[l])
        x = x + 0.5 * _rms_norm(_swiglu(h, ffn1_w_gate[l], ffn1_w_up[l], ffn1_w_down[l]), ffn1_norm_post[l])
        h = _rms_norm(x, mix_norm_pre[l])
        rwkv_params = (rwkv_mu[l], rwkv_w0[l], rwkv_w_up[l], rwkv_a0[l], rwkv_a_up[l], rwkv_g_up[l],
                       rwkv_k_k[l], rwkv_k_a[l], rwkv_r_k[l], rwkv_ln_w[l], rwkv_ln_b[l])
        diff_params = (diff_lambda_q1[l], diff_lambda_k1[l], diff_lambda_q2[l], diff_lambda_k2[l], diff_subln[l])
        lambda_init = 0.8 - 0.6 * math.exp(-0.3 * l)
        m = _mixer(h, w_in[l], rwkv_params, diff_params, w_branch[l], w_out[l], lambda_init)
        x = x + _rms_norm(m, mix_norm_post[l])
        h = _rms_norm(x, ffn2_norm_pre[l])
        x = x + 0.5 * _rms_norm(_swiglu(h, ffn2_w_gate[l], ffn2_w_up[l], ffn2_w_down[l]), ffn2_norm_post[l])
    return x
```

```python
import functools
import math

import jax
import jax.numpy as jnp
from jax import lax
from jax.experimental import pallas as pl
from jax.experimental.pallas import tpu as pltpu

F32 = jnp.float32
BF16 = jnp.bfloat16

D_MODEL = 1024
D_FF = 2816
DSA_HEADS = 4
DSA_HEAD_DIM = 128
IDX_HEADS = 8
IDX_DIM = 64
TOPK_MAX = 256
RWKV_HEADS = 8
RWKV_HEAD_DIM = 64
RWKV_WIDTH = RWKV_HEADS * RWKV_HEAD_DIM
DECAY_LORA = 64
AAA_LORA = 64
GATE_LORA = 128
RWKV_GN_EPS = 64e-5
DIFF_HEADS = 4
DIFF_QK_DIM = 64
DIFF_V_DIM = 128
N_BRANCH = 3
BRANCH_WIDTH = 512
NORM_EPS = 1e-6

_A = DSA_HEADS * DSA_HEAD_DIM
_DSA_W = 3 * _A + IDX_HEADS * IDX_DIM + IDX_DIM + IDX_HEADS
_RWKV_W = 3 * RWKV_WIDTH + DECAY_LORA + AAA_LORA + GATE_LORA
_DIFF_W = 3 * 512
_GATE_W = N_BRANCH * D_MODEL

ZATT_W = 3840
ZR_W = 1920

LANE = 128
VMEM_LIMIT = 56 * 1024 * 1024
NEG_BIG = -1e30
INT_MIN = -2 ** 31

RWKV_CHUNK = 64


def _cparams(sem):
    return pltpu.CompilerParams(dimension_semantics=sem, vmem_limit_bytes=VMEM_LIMIT)


def _rms(x, g):
    return x * lax.rsqrt(jnp.mean(x * x, axis=-1, keepdims=True) + NORM_EPS) * g


def _dot(a, b):
    return jnp.dot(a, b, preferred_element_type=F32)


def _dot_nt(a, b):
    return lax.dot_general(a, b, (((1,), (1,)), ((), ())), preferred_element_type=F32)


def _dot_tn(a, b):
    return lax.dot_general(a, b, (((0,), (0,)), ((), ())), preferred_element_type=F32)


def _ffn_body(x_ref, gpre_ref, gpost_ref, wg_ref, wu_ref, wd_ref, o_ref, h_ref, acc_ref, *, nf):
    x = x_ref[...]
    h_ref[...] = _rms(x, gpre_ref[...]).astype(BF16)
    acc_ref[...] = jnp.zeros_like(acc_ref)

    def chunk(c, carry):
        h = h_ref[...]
        g = _dot(h, wg_ref[c])
        u = _dot(h, wu_ref[c])
        a = (g * jax.nn.sigmoid(g) * u).astype(BF16)
        acc_ref[...] += _dot(a, wd_ref[c])
        return carry

    lax.fori_loop(0, nf, chunk, 0)
    o_ref[...] = x + 0.5 * _rms(acc_ref[...], gpost_ref[...])


def _ffn(x2, g_pre, g_post, wg, wu, wd, *, tm):
    T, D = x2.shape
    nf, _, tf = wg.shape
    const3 = lambda i: (0, 0, 0)
    return pl.pallas_call(
        functools.partial(_ffn_body, nf=nf),
        grid=(T // tm,),
        in_specs=[
            pl.BlockSpec((tm, D), lambda i: (i, 0)),
            pl.BlockSpec((1, D), lambda i: (0, 0)),
            pl.BlockSpec((1, D), lambda i: (0, 0)),
            pl.BlockSpec((nf, D, tf), const3, pipeline_mode=pl.Buffered(1)),
            pl.BlockSpec((nf, D, tf), const3, pipeline_mode=pl.Buffered(1)),
            pl.BlockSpec((nf, tf, D), const3, pipeline_mode=pl.Buffered(1)),
        ],
        out_specs=pl.BlockSpec((tm, D), lambda i: (i, 0)),
        out_shape=jax.ShapeDtypeStruct((T, D), F32),
        scratch_shapes=[pltpu.VMEM((tm, D), BF16), pltpu.VMEM((tm, D), F32)],
        compiler_params=_cparams(("arbitrary",)),
    )(x2, g_pre, g_post, wg, wu, wd)


def _ffn_weights(w_gate, w_up, w_down, tf):
    D, F = w_gate.shape
    nf = F // tf
    wg = w_gate.astype(BF16).reshape(D, nf, tf).transpose(1, 0, 2)
    wu = w_up.astype(BF16).reshape(D, nf, tf).transpose(1, 0, 2)
    wd = w_down.astype(BF16).reshape(nf, tf, D)
    return wg, wu, wd


def _proj_body(x_ref, g_ref, w_ref, o_ref, h_ref):
    @pl.when(pl.program_id(1) == 0)
    def _():
        h_ref[...] = _rms(x_ref[...], g_ref[...]).astype(BF16)

    o_ref[...] = _dot(h_ref[...], w_ref[...]).astype(o_ref.dtype)


def _proj(x2, g, w, out_dtype, *, tm, tn):
    T, D = x2.shape
    N = w.shape[1]
    return pl.pallas_call(
        _proj_body,
        grid=(T // tm, N // tn),
        in_specs=[
            pl.BlockSpec((tm, D), lambda i, j: (i, 0)),
            pl.BlockSpec((1, D), lambda i, j: (0, 0)),
            pl.BlockSpec((D, tn), lambda i, j: (0, j)),
        ],
        out_specs=pl.BlockSpec((tm, tn), lambda i, j: (i, j)),
        out_shape=jax.ShapeDtypeStruct((T, N), out_dtype),
        scratch_shapes=[pltpu.VMEM((tm, D), BF16)],
        compiler_params=_cparams(("arbitrary", "arbitrary")),
    )(x2, g, w)


def _split_w_in(w_in):
    D = w_in.shape[0]
    c = 0
    dsa = w_in[:, c:c + _DSA_W]; c += _DSA_W
    rw = w_in[:, c:c + _RWKV_W]; c += _RWKV_W
    df = w_in[:, c:c + _DIFF_W]; c += _DIFF_W
    gate = w_in[:, c:c + _GATE_W]
    qkvqi = dsa[:, :4 * _A]
    ki = dsa[:, 4 * _A:4 * _A + IDX_DIM]
    wi = dsa[:, 4 * _A + IDX_DIM:]
    w_att = jnp.concatenate(
        [qkvqi, df, ki, jnp.zeros((D, ZATT_W - 4 * _A - _DIFF_W - IDX_DIM), F32)], axis=1).astype(BF16)
    w_r = jnp.concatenate(
        [rw, wi, jnp.zeros((D, ZR_W - _RWKV_W - IDX_HEADS), F32)], axis=1).astype(BF16)
    return w_att, w_r, gate.astype(BF16)


def _diff_body(q_ref, kt_ref, v_ref, lq1_ref, lk1_ref, lq2_ref, lk2_ref, sub_ref, o_ref,
               m_ref, l_ref, acc_ref, *, tq, tk, lambda_init):
    i = pl.program_id(2)
    q = q_ref[...] * jnp.asarray(DIFF_QK_DIM ** -0.5, BF16)
    qs = (q[:, :DIFF_QK_DIM], q[:, DIFF_QK_DIM:])
    m_ref[...] = jnp.full_like(m_ref, NEG_BIG)
    l_ref[...] = jnp.zeros_like(l_ref)
    acc_ref[...] = jnp.zeros_like(acc_ref)
    n_full = (i * tq) // tk

    def step(c, masked):
        kt = kt_ref[c]
        v = v_ref[pl.ds(pl.multiple_of(c * tk, tk), tk), :]
        if masked:
            row = i * tq + lax.broadcasted_iota(jnp.int32, (tq, tk), 0)
            col = c * tk + lax.broadcasted_iota(jnp.int32, (tq, tk), 1)
            keep = col <= row
        for mp in range(2):
            s = _dot(qs[mp], kt[mp * DIFF_QK_DIM:(mp + 1) * DIFF_QK_DIM, :])
            if masked:
                s = jnp.where(keep, s, NEG_BIG)
            m_old = m_ref[mp]
            m_new = jnp.maximum(m_old, jnp.max(s, axis=1, keepdims=True))
            alpha = jnp.exp(m_old - m_new)
            p = jnp.exp(s - m_new)
            l_ref[mp] = alpha * l_ref[mp] + jnp.sum(p, axis=1, keepdims=True)
            acc_ref[mp] = alpha * acc_ref[mp] + _dot(p.astype(BF16), v)
            m_ref[mp] = m_new

    def full_step(c, carry):
        step(c, False)
        return carry

    lax.fori_loop(0, n_full, full_step, 0)
    step(n_full, True)

    lam = (jnp.exp(jnp.sum(lq1_ref[...] * lk1_ref[...], axis=1, keepdims=True))
           - jnp.exp(jnp.sum(lq2_ref[...] * lk2_ref[...], axis=1, keepdims=True)) + lambda_init)
    o = acc_ref[0] / l_ref[0] - lam * (acc_ref[1] / l_ref[1])
    y = _rms(o, sub_ref[...]) * (1.0 - lambda_init)
    o_ref[...] = y.astype(o_ref.dtype)


def _diff_attn(zatt3, kt_diff, lq1, lk1, lq2, lk2, subln, lambda_init, *, tq, tk):
    B, L, _ = zatt3.shape
    nkt = L // tk
    qcol = (4 * _A) // LANE
    vcol = (4 * _A + 1024) // LANE
    vec = lambda n: pl.BlockSpec((1, n), lambda b, h, i: (0, 0))
    return pl.pallas_call(
        functools.partial(_diff_body, tq=tq, tk=tk, lambda_init=lambda_init),
        grid=(B, DIFF_HEADS, L // tq),
        in_specs=[
            pl.BlockSpec((None, tq, LANE), lambda b, h, i: (b, i, qcol + h)),
            pl.BlockSpec((None, None, nkt, LANE, tk), lambda b, h, i: (b, h, 0, 0, 0)),
            pl.BlockSpec((None, L, LANE), lambda b, h, i: (b, 0, vcol + h)),
            vec(DIFF_QK_DIM), vec(DIFF_QK_DIM), vec(DIFF_QK_DIM), vec(DIFF_QK_DIM), vec(DIFF_V_DIM),
        ],
        out_specs=pl.BlockSpec((None, tq, LANE), lambda b, h, i: (b, i, h)),
        out_shape=jax.ShapeDtypeStruct((B, L, DIFF_HEADS * DIFF_V_DIM), BF16),
        scratch_shapes=[pltpu.VMEM((2, tq, 1), F32), pltpu.VMEM((2, tq, 1), F32),
                        pltpu.VMEM((2, tq, DIFF_V_DIM), F32)],
        compiler_params=_cparams(("arbitrary", "arbitrary", "arbitrary")),
    )(zatt3, kt_diff, zatt3, lq1, lk1, lq2, lk2, subln)


def _sortable(x):
    b = pltpu.bitcast(x, jnp.int32)
    return b ^ ((b >> 31) & jnp.int32(0x7FFFFFFF))


def _lane_fold(x):
    n = x.shape[1] // LANE
    acc = x[:, :LANE]
    for j in range(1, n):
        acc = acc + x[:, j * LANE:(j + 1) * LANE]
    return acc


def _dsa_body(q_ref, qi_ref, wi_ref, kt_ref, v_ref, kit_ref, su_ref, o_ref,
              keys_ref, m_ref, l_ref, acc_ref, *, tq, tk, topk):
    i = pl.program_id(1)
    nkt = ((i + 1) * tq + tk - 1) // tk
    row = i * tq + lax.broadcasted_iota(jnp.int32, (tq, tk), 0)
    col0 = lax.broadcasted_iota(jnp.int32, (tq, tk), 1)

    qi = qi_ref[...]
    w = wi_ref[...] * (IDX_HEADS ** -0.5 * IDX_DIM ** -0.5)
    qis = [qi[:, h * IDX_DIM:(h + 1) * IDX_DIM] for h in range(IDX_HEADS)]
    ws = [w[:, h:h + 1] for h in range(IDX_HEADS)]

    def score_tile(c, carry):
        kit = kit_ref[c]
        sc = jnp.zeros((tq, tk), F32)
        for h in range(IDX_HEADS):
            sc = sc + jnp.maximum(_dot(qis[h], kit), 0.0) * ws[h]
        key = _sortable(sc)
        keys_ref[c] = jnp.where(c * tk + col0 <= row, key, INT_MIN)
        return carry

    lax.fori_loop(0, nkt, score_tile, 0)

    def count_where(pred):
        def tile(c, acc):
            return acc + _lane_fold(jnp.where(pred(keys_ref[c]), 1.0, 0.0))
        acc = lax.fori_loop(0, nkt, tile, jnp.zeros((tq, LANE), F32))
        return jnp.sum(acc, axis=1, keepdims=True)

    def bit_pass(p, t):
        cand = t ^ lax.shift_left(jnp.int32(1), 31 - p)
        cnt = count_where(lambda k: k >= cand)
        return jnp.where(cnt >= topk, cand, t)

    thr = lax.fori_loop(0, 32, bit_pass, jnp.full((tq, 1), INT_MIN, jnp.int32))
    need = topk - count_where(lambda k: k > thr)

    q = q_ref[...]
    qh = [q[:, h * DSA_HEAD_DIM:(h + 1) * DSA_HEAD_DIM] for h in range(DSA_HEADS)]
    m_ref[...] = jnp.full_like(m_ref, NEG_BIG)
    l_ref[...] = jnp.zeros_like(l_ref)
    acc_ref[...] = jnp.zeros_like(acc_ref)
    scale = DSA_HEAD_DIM ** -0.5

    def attn_tile(c, before):
        key = keys_ref[c]
        eq = key == thr
        eqf = jnp.where(eq, 1.0, 0.0)
        rank = before + _dot(eqf.astype(BF16), su_ref[...])
        tie_ok = jnp.where(eq, jnp.where(rank < need, 0.0, NEG_BIG), NEG_BIG)
        bias = jnp.where(key > thr, 0.0, tie_ok)
        bias = jnp.where(c * tk + col0 <= row, bias, NEG_BIG)
        kt = kt_ref[c]
        v = v_ref[pl.ds(pl.multiple_of(c * tk, tk), tk), :]
        for h in range(DSA_HEADS):
            sl = slice(h * DSA_HEAD_DIM, (h + 1) * DSA_HEAD_DIM)
            s = _dot(qh[h], kt[sl, :]) * scale + bias
            m_old = m_ref[h]
            m_new = jnp.maximum(m_old, jnp.max(s, axis=1, keepdims=True))
            alpha = jnp.exp(m_old - m_new)
            p = jnp.exp(s - m_new)
            l_ref[h] = alpha * l_ref[h] + jnp.sum(p, axis=1, keepdims=True)
            acc_ref[h] = alpha * acc_ref[h] + _dot(p.astype(BF16), v[:, sl])
            m_ref[h] = m_new
        return before + jnp.sum(eqf, axis=1, keepdims=True)

    lax.fori_loop(0, nkt, attn_tile, jnp.zeros((tq, 1), F32))
    for h in range(DSA_HEADS):
        sl = slice(h * DSA_HEAD_DIM, (h + 1) * DSA_HEAD_DIM)
        o_ref[:, sl] = (acc_ref[h] / l_ref[h]).astype(o_ref.dtype)


def _dsa(zatt3, zr3, kt_dsa, kit, *, tq, tk):
    B, L, _ = zatt3.shape
    nkt = L // tk
    topk = min(TOPK_MAX, L // 4)
    su = (lax.broadcasted_iota(jnp.int32, (tk, tk), 0)
          < lax.broadcasted_iota(jnp.int32, (tk, tk), 1)).astype(BF16)
    once = pl.Buffered(1)
    return pl.pallas_call(
        functools.partial(_dsa_body, tq=tq, tk=tk, topk=topk),
        grid=(B, L // tq),
        in_specs=[
            pl.BlockSpec((None, tq, _A), lambda b, i: (b, i, 0)),
            pl.BlockSpec((None, tq, _A), lambda b, i: (b, i, 3)),
            pl.BlockSpec((None, tq, LANE), lambda b, i: (b, i, _RWKV_W // LANE)),
            pl.BlockSpec((None, nkt, _A, tk), lambda b, i: (b, 0, 0, 0), pipeline_mode=once),
            pl.BlockSpec((None, L, _A), lambda b, i: (b, 0, 2), pipeline_mode=once),
            pl.BlockSpec((None, nkt, IDX_DIM, tk), lambda b, i: (b, 0, 0, 0), pipeline_mode=once),
            pl.BlockSpec((tk, tk), lambda b, i: (0, 0), pipeline_mode=once),
        ],
        out_specs=pl.BlockSpec((None, tq, _A), lambda b, i: (b, i, 0)),
        out_shape=jax.ShapeDtypeStruct((B, L, _A), BF16),
        scratch_shapes=[pltpu.VMEM((nkt, tq, tk), jnp.int32),
                        pltpu.VMEM((DSA_HEADS, tq, 1), F32), pltpu.VMEM((DSA_HEADS, tq, 1), F32),
                        pltpu.VMEM((DSA_HEADS, tq, DSA_HEAD_DIM), F32)],
        compiler_params=_cparams(("arbitrary", "arbitrary")),
    )(zatt3, zatt3, zr3, kt_dsa, zatt3, kit, su)


def _tiled_transpose(z, tk):
    B, L, C = z.shape
    return z.reshape(B, L // tk, tk, C).transpose(0, 1, 3, 2)


def _split_dot(m, x):
    hi = x.astype(BF16)
    lo = (x - hi.astype(F32)).astype(BF16)
    return _dot(m, hi) + _dot(m, lo)


def _rwkv_body(z_ref, mu_ref, w0_ref, wa_ref, a0_ref, gup_ref, kk_ref, ka_ref, rk_ref,
               lnw_ref, lnb_ref, seg_ref, tri_ref, o_ref, prev_ref, s_ref, *, C):
    c = pl.program_id(1)
    W = RWKV_WIDTH

    @pl.when(c == 0)
    def _():
        prev_ref[...] = jnp.zeros_like(prev_ref)
        s_ref[...] = jnp.zeros_like(s_ref)

    z = z_ref[...]
    rows = lax.broadcasted_iota(jnp.int32, z.shape, 0)
    zp = jnp.where(rows == 0, prev_ref[0:1, :], pltpu.roll(z, 1, 0))
    prev_ref[0:1, :] = z[C - 1:C, :]
    zs = z + (zp - z) * mu_ref[...]
    r = zs[:, 0:W]
    k = zs[:, W:2 * W]
    v = zs[:, 2 * W:3 * W]
    wa = zs[:, 3 * W:3 * W + LANE]
    gc = zs[:, 3 * W + LANE:3 * W + 2 * LANE]

    lane = lax.broadcasted_iota(jnp.int32, (C, LANE), 1)
    first = lane < RWKV_HEAD_DIM
    wa_t = jnp.where(first, jnp.tanh(wa), wa)
    lora = _dot(wa_t.astype(BF16), wa_ref[...])
    y = w0_ref[...] + lora[:, :W]
    w_log = -(jnp.maximum(-y, 0.0) + jnp.log(1.0 + jnp.exp(-jnp.abs(y)))) - 0.5
    lw = -jnp.exp(w_log)
    a = jax.nn.sigmoid(a0_ref[...] + lora[:, W:])
    g = _dot(jax.nn.sigmoid(gc).astype(BF16), gup_ref[...])
    seg = seg_ref[...]
    kk = k * kk_ref[...]
    kk = kk / jnp.maximum(jnp.sqrt(_dot((kk * kk).astype(BF16), seg)), 1e-12)
    k2 = k * (1.0 + (a - 1.0) * ka_ref[...])

    cum = _split_dot(tri_ref[...], lw)
    p_inv = jnp.exp(-cum)
    r_t = (r * jnp.exp(cum)).astype(BF16)
    k_t = (k2 * p_inv).astype(BF16)
    a_t = (-kk * jnp.exp(cum - lw)).astype(BF16)
    b_t = (kk * a * p_inv).astype(BF16)
    p_end = jnp.exp(cum[C - 1:C, :])
    vb = v.astype(BF16)

    zero = jnp.zeros((C, LANE), BF16)

    def stack(x):
        return jnp.concatenate([jnp.where(first, x, zero), jnp.where(first, zero, x)], axis=0)

    ri = lax.broadcasted_iota(jnp.int32, (2 * C, 2 * C), 0)
    ci = lax.broadcasted_iota(jnp.int32, (2 * C, 2 * C), 1)
    strict = ci < ri
    incl = ci <= ri
    eye = jnp.where(ci == ri, 1.0, 0.0)

    ys = []
    for p in range(RWKV_HEADS // 2):
        sl = slice(p * LANE, (p + 1) * LANE)
        v_st = stack(vb[:, sl])
        ar = jnp.concatenate([stack(a_t[:, sl]), stack(r_t[:, sl])], axis=0)
        bk = jnp.concatenate([stack(b_t[:, sl]), stack(k_t[:, sl])], axis=0)
        m = _dot_nt(ar, bk)
        l_ab = jnp.where(strict, m[:2 * C, :2 * C], 0.0)
        l_ak = jnp.where(strict, m[:2 * C, 2 * C:], 0.0)
        m_rb = jnp.where(incl, m[2 * C:, :2 * C], 0.0)
        m_rk = jnp.where(incl, m[2 * C:, 2 * C:], 0.0)
        t_inv = eye + l_ab
        pw = l_ab
        for _ in range(int(math.log2(C)) - 1):
            pwb = pw.astype(BF16)
            pw = _dot(pwb, pwb)
            t_inv = t_inv + _dot(t_inv.astype(BF16), pw.astype(BF16))
        s_old = s_ref[p]
        xs = _dot_nt(jnp.concatenate([a_t[:, sl], r_t[:, sl]], axis=0), s_old.astype(BF16))
        x_st = stack(xs[:C].astype(BF16)).astype(F32) + _dot(l_ak.astype(BF16), v_st)
        u_st = _dot(t_inv.astype(BF16), x_st.astype(BF16)).astype(BF16)
        uv = jnp.concatenate([u_st, v_st], axis=0)
        y_st = _dot(jnp.concatenate([m_rb, m_rk], axis=1).astype(BF16), uv)
        ys.append(xs[C:] + y_st[:C] + y_st[C:])
        s_ref[p] = (s_old + _dot_tn(uv, bk)) * p_end[:, sl]
    yy = jnp.concatenate(ys, axis=1)

    inv_n = 1.0 / RWKV_HEAD_DIM
    mean = _dot(yy.astype(BF16), seg) * inv_n
    d = yy - mean
    var = _dot((d * d).astype(BF16), seg) * inv_n
    yn = d * lax.rsqrt(var + RWKV_GN_EPS) * lnw_ref[...] + lnb_ref[...]
    bonus = _dot((r * k2 * rk_ref[...]).astype(BF16), seg) * v
    o_ref[...] = ((yn + bonus) * g).astype(o_ref.dtype)


def _rwkv(zr3, mu, w0, w_up, a0, a_up, g_up, k_k, k_a, r_k, ln_w, ln_b):
    B, L, _ = zr3.shape
    C = RWKV_CHUNK
    W = RWKV_WIDTH
    row = lambda t: t.reshape(1, -1)
    wa = jnp.zeros((LANE, 2 * W), F32)
    wa = wa.at[:DECAY_LORA, :W].set(w_up).at[DECAY_LORA:, W:].set(a_up).astype(BF16)
    hid = jnp.arange(W) // RWKV_HEAD_DIM
    seg = (hid[:, None] == hid[None, :]).astype(BF16)
    tri = (jnp.arange(C)[None, :] <= jnp.arange(C)[:, None]).astype(BF16)
    c2 = lambda b, c: (0, 0)
    vec = lambda n: pl.BlockSpec((1, n), c2)
    return pl.pallas_call(
        functools.partial(_rwkv_body, C=C),
        grid=(B, L // C),
        in_specs=[
            pl.BlockSpec((None, C, _RWKV_W), lambda b, c: (b, c, 0)),
            vec(_RWKV_W), vec(W),
            pl.BlockSpec((LANE, 2 * W), c2),
            vec(W),
            pl.BlockSpec((GATE_LORA, W), c2),
            vec(W), vec(W), vec(W), vec(W), vec(W),
            pl.BlockSpec((W, W), c2),
            pl.BlockSpec((C, C), c2),
        ],
        out_specs=pl.BlockSpec((None, C, W), lambda b, c: (b, c, 0)),
        out_shape=jax.ShapeDtypeStruct((B, L, W), BF16),
        scratch_shapes=[pltpu.VMEM((8, _RWKV_W), F32),
                        pltpu.VMEM((RWKV_HEADS // 2, LANE, LANE), F32)],
        compiler_params=_cparams(("arbitrary", "arbitrary")),
    )(zr3, row(mu), row(w0), wa, row(a0), g_up.astype(BF16), row(k_k), row(k_a), row(r_k),
      row(ln_w), row(ln_b), seg, tri)


def _merge_body(x_ref, ya_ref, yb_ref, yc_ref, g0_ref, g1_ref, g2_ref, wb_ref, wo_ref, gpost_ref, o_ref):
    m = (jax.nn.sigmoid(g0_ref[...].astype(F32)) * _dot(ya_ref[...], wb_ref[0])
         + jax.nn.sigmoid(g1_ref[...].astype(F32)) * _dot(yb_ref[...], wb_ref[1])
         + jax.nn.sigmoid(g2_ref[...].astype(F32)) * _dot(yc_ref[...], wb_ref[2]))
    y = _dot(m.astype(BF16), wo_ref[...])
    o_ref[...] = x_ref[...] + _rms(y, gpost_ref[...])


def _merge(x2, ya, yb, yc, zg, wb, wo, g_post, *, tm):
    T, D = x2.shape
    Wb = ya.shape[1]
    tok = lambda w, j: pl.BlockSpec((tm, w), lambda i: (i, j))
    return pl.pallas_call(
        _merge_body,
        grid=(T // tm,),
        in_specs=[
            tok(D, 0), tok(Wb, 0), tok(Wb, 0), tok(Wb, 0),
            tok(D, 0), tok(D, 1), tok(D, 2),
            pl.BlockSpec((N_BRANCH, Wb, D), lambda i: (0, 0, 0)),
            pl.BlockSpec((D, D), lambda i: (0, 0)),
            pl.BlockSpec((1, D), lambda i: (0, 0)),
        ],
        out_specs=tok(D, 0),
        out_shape=jax.ShapeDtypeStruct((T, D), F32),
        compiler_params=_cparams(("arbitrary",)),
    )(x2, ya, yb, yc, zg, zg, zg, wb, wo, g_post)


def _tiles(B, L):
    T = B * L
    return dict(
        tm_ffn=min(512, T), tf=256,
        tm_proj=min(1024, T),
        tm_merge=min(512, T),
        tq_dsa=min(128, L), tk_dsa=min(512, L),
        tq_diff=min(256, L), tk_diff=min(512, L),
    )


def _mixer(x2, B, L, l, cfg, mix_norm_pre, mix_norm_post, w_in, rwkv, diffp, w_branch, w_out):
    T, D = x2.shape
    g_pre = mix_norm_pre.reshape(1, D)
    w_att, w_r, w_gate = _split_w_in(w_in)
    zatt = _proj(x2, g_pre, w_att, BF16, tm=cfg["tm_proj"], tn=768)
    zr = _proj(x2, g_pre, w_r, F32, tm=cfg["tm_proj"], tn=640)
    zg = _proj(x2, g_pre, w_gate, F32, tm=cfg["tm_proj"], tn=1024)
    zatt3 = zatt.reshape(B, L, ZATT_W)
    zr3 = zr.reshape(B, L, ZR_W)

    tk = cfg["tk_dsa"]
    kt_dsa = _tiled_transpose(zatt3[:, :, _A:2 * _A], tk)
    ki0 = 4 * _A + _DIFF_W
    kit = _tiled_transpose(zatt3[:, :, ki0:ki0 + IDX_DIM], tk)
    ya = _dsa(zatt3, zr3, kt_dsa, kit, tq=cfg["tq_dsa"], tk=tk)

    yb = _rwkv(zr3, *rwkv)

    tkd = cfg["tk_diff"]
    kd = zatt3[:, :, 4 * _A + 512:4 * _A + 1024]
    kt_diff = kd.reshape(B, L // tkd, tkd, DIFF_HEADS, LANE).transpose(0, 3, 1, 4, 2)
    lq1, lk1, lq2, lk2, subln = diffp
    lambda_init = 0.8 - 0.6 * math.exp(-0.3 * l)
    rowv = lambda t: t.reshape(1, -1)
    yc = _diff_attn(zatt3, kt_diff, rowv(lq1), rowv(lk1), rowv(lq2), rowv(lk2), rowv(subln),
                    lambda_init, tq=cfg["tq_diff"], tk=tkd)

    return _merge(x2, ya.reshape(T, -1), yb.reshape(T, -1), yc.reshape(T, -1), zg,
                  w_branch.astype(BF16), w_out.astype(BF16), mix_norm_post.reshape(1, D),
                  tm=cfg["tm_merge"])


def kernel(x, ffn1_norm_pre, ffn1_norm_post, ffn1_w_gate, ffn1_w_up, ffn1_w_down,
           mix_norm_pre, mix_norm_post, w_in, rwkv_mu, rwkv_w0, rwkv_w_up, rwkv_a0, rwkv_a_up,
           rwkv_g_up, rwkv_k_k, rwkv_k_a, rwkv_r_k, rwkv_ln_w, rwkv_ln_b,
           diff_lambda_q1, diff_lambda_k1, diff_lambda_q2, diff_lambda_k2, diff_subln,
           w_branch, w_out, ffn2_norm_pre, ffn2_norm_post, ffn2_w_gate, ffn2_w_up, ffn2_w_down):
    B, L, D = x.shape
    depth = w_in.shape[0]
    cfg = _tiles(B, L)
    x2 = x.reshape(B * L, D)
    for l in range(depth):
        x2 = _ffn(x2, ffn1_norm_pre[l].reshape(1, D), ffn1_norm_post[l].reshape(1, D),
                  *_ffn_weights(ffn1_w_gate[l], ffn1_w_up[l], ffn1_w_down[l], cfg["tf"]),
                  tm=cfg["tm_ffn"])
        rwkv = (rwkv_mu[l], rwkv_w0[l], rwkv_w_up[l], rwkv_a0[l], rwkv_a_up[l], rwkv_g_up[l],
                rwkv_k_k[l], rwkv_k_a[l], rwkv_r_k[l], rwkv_ln_w[l], rwkv_ln_b[l])
        diffp = (diff_lambda_q1[l], diff_lambda_k1[l], diff_lambda_q2[l], diff_lambda_k2[l],
                 diff_subln[l])
        x2 = _mixer(x2, B, L, l, cfg, mix_norm_pre[l], mix_norm_post[l], w_in[l], rwkv, diffp,
                    w_branch[l], w_out[l])
        x2 = _ffn(x2, ffn2_norm_pre[l].reshape(1, D), ffn2_norm_post[l].reshape(1, D),
                  *_ffn_weights(ffn2_w_gate[l], ffn2_w_up[l], ffn2_w_down[l], cfg["tf"]),
                  tm=cfg["tm_ffn"])
    return x2.reshape(B, L, D)
```

```python
import functools
import math

import jax
import jax.numpy as jnp
from jax import lax
from jax.experimental import pallas as pl
from jax.experimental.pallas import tpu as pltpu

F32 = jnp.float32
BF16 = jnp.bfloat16

D_MODEL = 1024
D_FF = 2816
DSA_HEADS = 4
DSA_HEAD_DIM = 128
IDX_HEADS = 8
IDX_DIM = 64
TOPK_MAX = 256
RWKV_HEADS = 8
RWKV_HEAD_DIM = 64
RWKV_WIDTH = RWKV_HEADS * RWKV_HEAD_DIM
DECAY_LORA = 64
AAA_LORA = 64
GATE_LORA = 128
RWKV_GN_EPS = 64e-5
DIFF_HEADS = 4
DIFF_QK_DIM = 64
DIFF_V_DIM = 128
N_BRANCH = 3
BRANCH_WIDTH = 512
NORM_EPS = 1e-6

_A = DSA_HEADS * DSA_HEAD_DIM
_DSA_W = 3 * _A + IDX_HEADS * IDX_DIM + IDX_DIM + IDX_HEADS
_RWKV_W = 3 * RWKV_WIDTH + DECAY_LORA + AAA_LORA + GATE_LORA
_DIFF_W = 3 * 512
_GATE_W = N_BRANCH * D_MODEL

LANE = 128
SUBLANE = 8
VMEM_LIMIT = 56 * 1024 * 1024
NEG_BIG = -1e30
INT_MIN = -2 ** 31

ZT_ROWS = 5 * _A
ZN_W = 2 * _A + LANE

RWKV_CHUNK = 64
ATT_SUB = 128


def _cparams(sem):
    return pltpu.CompilerParams(dimension_semantics=sem, vmem_limit_bytes=VMEM_LIMIT)


def _rms(x, g):
    return x * lax.rsqrt(jnp.mean(x * x, axis=-1, keepdims=True) + NORM_EPS) * g


def _dot(a, b):
    return jnp.dot(a, b, preferred_element_type=F32)


def _dot_nt(a, b):
    return lax.dot_general(a, b, (((1,), (1,)), ((), ())), preferred_element_type=F32)


def _dot_tn(a, b):
    return lax.dot_general(a, b, (((0,), (0,)), ((), ())), preferred_element_type=F32)


def _ffn_body(x_ref, gpre_ref, gpost_ref, wg_ref, wu_ref, wd_ref, o_ref, h_ref, acc_ref, *, nf):
    x = x_ref[...]
    h_ref[...] = _rms(x, gpre_ref[...]).astype(BF16)
    acc_ref[...] = jnp.zeros_like(acc_ref)

    def chunk(c, carry):
        h = h_ref[...]
        g = _dot(h, wg_ref[c])
        u = _dot(h, wu_ref[c])
        a = (g * jax.nn.sigmoid(g) * u).astype(BF16)
        acc_ref[...] += _dot(a, wd_ref[c])
        return carry

    lax.fori_loop(0, nf, chunk, 0)
    o_ref[...] = x + 0.5 * _rms(acc_ref[...], gpost_ref[...])


def _ffn(x2, g_pre, g_post, wg, wu, wd, *, tm):
    T, D = x2.shape
    nf, _, tf = wg.shape
    const3 = lambda i: (0, 0, 0)
    return pl.pallas_call(
        functools.partial(_ffn_body, nf=nf),
        grid=(T // tm,),
        in_specs=[
            pl.BlockSpec((tm, D), lambda i: (i, 0)),
            pl.BlockSpec((1, D), lambda i: (0, 0)),
            pl.BlockSpec((1, D), lambda i: (0, 0)),
            pl.BlockSpec((nf, D, tf), const3, pipeline_mode=pl.Buffered(1)),
            pl.BlockSpec((nf, D, tf), const3, pipeline_mode=pl.Buffered(1)),
            pl.BlockSpec((nf, tf, D), const3, pipeline_mode=pl.Buffered(1)),
        ],
        out_specs=pl.BlockSpec((tm, D), lambda i: (i, 0)),
        out_shape=jax.ShapeDtypeStruct((T, D), F32),
        scratch_shapes=[pltpu.VMEM((tm, D), BF16), pltpu.VMEM((tm, D), F32)],
        compiler_params=_cparams(("arbitrary",)),
    )(x2, g_pre, g_post, wg, wu, wd)


def _ffn_weights(w_gate, w_up, w_down, tf):
    D, F = w_gate.shape
    nf = F // tf
    wg = w_gate.astype(BF16).reshape(D, nf, tf).transpose(1, 0, 2)
    wu = w_up.astype(BF16).reshape(D, nf, tf).transpose(1, 0, 2)
    wd = w_down.astype(BF16).reshape(nf, tf, D)
    return wg, wu, wd


def _proj_body(x_ref, g_ref, w_ref, o_ref, h_ref):
    @pl.when(pl.program_id(1) == 0)
    def _():
        h_ref[...] = _rms(x_ref[...], g_ref[...]).astype(BF16)

    o_ref[...] = _dot(h_ref[...], w_ref[...]).astype(o_ref.dtype)


def _proj(x2, g, w, out_dtype, *, tm, tn):
    T, D = x2.shape
    N = w.shape[1]
    return pl.pallas_call(
        _proj_body,
        grid=(T // tm, N // tn),
        in_specs=[
            pl.BlockSpec((tm, D), lambda i, j: (i, 0)),
            pl.BlockSpec((1, D), lambda i, j: (0, 0)),
            pl.BlockSpec((D, tn), lambda i, j: (0, j)),
        ],
        out_specs=pl.BlockSpec((tm, tn), lambda i, j: (i, j)),
        out_shape=jax.ShapeDtypeStruct((T, N), out_dtype),
        scratch_shapes=[pltpu.VMEM((tm, D), BF16)],
        compiler_params=_cparams(("arbitrary", "arbitrary")),
    )(x2, g, w)


def _proj_t_body(x_ref, g_ref, wt_ref, sc_ref, wit_ref, o_ref, wi_ref, h_ref):
    @pl.when(pl.program_id(1) == 0)
    def _():
        h = _rms(x_ref[...], g_ref[...]).astype(BF16)
        h_ref[...] = h
        wi_ref[...] = _dot_nt(wit_ref[...], h)

    o_ref[...] = (_dot_nt(wt_ref[...], h_ref[...]) * sc_ref[...]).astype(o_ref.dtype)


def _proj_t(x2, g, wt, row_scale, wit, *, tm, tn):
    T, D = x2.shape
    N = wt.shape[0]
    return pl.pallas_call(
        _proj_t_body,
        grid=(T // tm, N // tn),
        in_specs=[
            pl.BlockSpec((tm, D), lambda i, j: (i, 0)),
            pl.BlockSpec((1, D), lambda i, j: (0, 0)),
            pl.BlockSpec((tn, D), lambda i, j: (j, 0)),
            pl.BlockSpec((tn, 1), lambda i, j: (j, 0)),
            pl.BlockSpec((IDX_HEADS, D), lambda i, j: (0, 0)),
        ],
        out_specs=[pl.BlockSpec((tn, tm), lambda i, j: (j, i)),
                   pl.BlockSpec((IDX_HEADS, tm), lambda i, j: (0, i))],
        out_shape=[jax.ShapeDtypeStruct((N, T), BF16), jax.ShapeDtypeStruct((IDX_HEADS, T), F32)],
        scratch_shapes=[pltpu.VMEM((tm, D), BF16)],
        compiler_params=_cparams(("arbitrary", "arbitrary")),
    )(x2, g, wt, row_scale, wit)


def _split_w_in(w_in):
    D = w_in.shape[0]
    c = 0
    dsa = w_in[:, c:c + _DSA_W]; c += _DSA_W
    rw = w_in[:, c:c + _RWKV_W]; c += _RWKV_W
    df = w_in[:, c:c + _DIFF_W]; c += _DIFF_W
    gate = w_in[:, c:c + _GATE_W]
    q, k, v, qi = (dsa[:, j * _A:(j + 1) * _A] for j in range(4))
    ki = dsa[:, 4 * _A:4 * _A + IDX_DIM]
    wi = dsa[:, 4 * _A + IDX_DIM:]
    dq, dk, dv = (df[:, j * 512:(j + 1) * 512] for j in range(3))
    wt = jnp.concatenate([q, v, qi, dq, dv], axis=1).T.astype(BF16)
    ones = jnp.ones((_A,), F32)
    log2e = math.log2(math.e)
    row_scale = jnp.concatenate([ones * (DSA_HEAD_DIM ** -0.5 * log2e), ones, ones,
                                 ones * (DIFF_QK_DIM ** -0.5 * log2e), ones]).reshape(ZT_ROWS, 1)
    wn = jnp.concatenate([k, dk, ki, jnp.zeros((D, LANE - IDX_DIM), F32)], axis=1).astype(BF16)
    return wt, row_scale, wi.T.astype(BF16), wn, rw.astype(BF16), gate.astype(BF16)


def _fold_rows(x, op):
    r, n = x.shape
    parts = x.reshape(r // SUBLANE, SUBLANE, n)
    while parts.shape[0] > 1:
        pairs = parts.reshape(parts.shape[0] // 2, 2, SUBLANE, n)
        parts = op(pairs[:, 0], pairs[:, 1])
    return parts[0]


def _col_max(x):
    return jnp.max(_fold_rows(x, jnp.maximum), axis=0, keepdims=True)


def _col_sum(x):
    return jnp.sum(_fold_rows(x, jnp.add), axis=0, keepdims=True)


def _diff_body(qt_ref, k_ref, vt_ref, lq1_ref, lk1_ref, lq2_ref, lk2_ref, sub_ref, o_ref,
               acc_ref, *, tq, tk, lambda_init):
    i = pl.program_id(1)
    qt = qt_ref[...]
    first = (lax.broadcasted_iota(jnp.int32, qt.shape, 0) & DIFF_QK_DIM) == 0
    zero = jnp.zeros_like(qt)
    qts = (jnp.where(first, qt, zero), jnp.where(first, zero, qt))
    acc_ref[...] = jnp.zeros_like(acc_ref)
    n_full = (i * tq) // tk

    def step(c, carry, masked):
        off = pl.multiple_of(c * tk, tk)
        k = k_ref[pl.ds(off, tk), :]
        if masked:
            key = c * tk + lax.broadcasted_iota(jnp.int32, (tk, tq), 0)
            qry = i * tq + lax.broadcasted_iota(jnp.int32, (tk, tq), 1)
            keep = key <= qry
        out = []
        for h in range(DIFF_HEADS):
            hs = slice(h * LANE, (h + 1) * LANE)
            vt = vt_ref[hs, pl.ds(off, tk)]
            for mp in range(2):
                j = 2 * h + mp
                m_old, l_old = carry[2 * j], carry[2 * j + 1]
                ss = []
                for r in range(tk // ATT_SUB):
                    rs = slice(r * ATT_SUB, (r + 1) * ATT_SUB)
                    s = _dot(k[rs, hs], qts[mp][hs, :])
                    ss.append(jnp.where(keep[rs], s, NEG_BIG) if masked else s)
                mx = functools.reduce(jnp.maximum, [_fold_rows(s, jnp.maximum) for s in ss])
                m_new = jnp.maximum(m_old, jnp.max(mx, axis=0, keepdims=True))
                alpha = jnp.exp2(m_old - m_new)
                ps = [jnp.exp2(s - m_new) for s in ss]
                sm = functools.reduce(jnp.add, [_fold_rows(p, jnp.add) for p in ps])
                l_new = alpha * l_old + jnp.sum(sm, axis=0, keepdims=True)
                pv = functools.reduce(jnp.add, [
                    _dot(vt[:, r * ATT_SUB:(r + 1) * ATT_SUB], p.astype(BF16)) for r, p in enumerate(ps)])
                acc_ref[j] = alpha * acc_ref[j] + pv
                out += [m_new, l_new]
        return tuple(out)

    init = (jnp.full((1, tq), NEG_BIG, F32), jnp.zeros((1, tq), F32)) * (2 * DIFF_HEADS)
    carry = lax.fori_loop(0, n_full, lambda c, cr: step(c, cr, False), init)
    fin = step(n_full, carry, True)

    lam = (jnp.exp(jnp.sum(lq1_ref[...] * lk1_ref[...], axis=1, keepdims=True))
           - jnp.exp(jnp.sum(lq2_ref[...] * lk2_ref[...], axis=1, keepdims=True)) + lambda_init)
    gain = sub_ref[...] * (1.0 - lambda_init)
    for h in range(DIFF_HEADS):
        o = acc_ref[2 * h] / fin[4 * h + 1] - lam * (acc_ref[2 * h + 1] / fin[4 * h + 3])
        ms = jnp.sum(o * o, axis=0, keepdims=True) * (1.0 / DIFF_V_DIM)
        o_ref[h * LANE:(h + 1) * LANE, :] = (o * lax.rsqrt(ms + NORM_EPS) * gain).astype(o_ref.dtype)


def _diff_attn(zt, zn3, lq1, lk1, lq2, lk2, subln, lambda_init, *, tq, tk):
    B, L, _ = zn3.shape
    T = B * L
    nq = L // tq
    vec = lambda n: pl.BlockSpec((1, n), lambda b, i: (0, 0))
    once = pl.Buffered(1)
    return pl.pallas_call(
        functools.partial(_diff_body, tq=tq, tk=tk, lambda_init=lambda_init),
        grid=(B, nq),
        in_specs=[
            pl.BlockSpec((_A, tq), lambda b, i: (3, b * nq + i)),
            pl.BlockSpec((None, L, _A), lambda b, i: (b, 0, 1), pipeline_mode=once),
            pl.BlockSpec((_A, L), lambda b, i: (4, b), pipeline_mode=once),
            vec(DIFF_QK_DIM), vec(DIFF_QK_DIM), vec(DIFF_QK_DIM), vec(DIFF_QK_DIM),
            pl.BlockSpec((DIFF_V_DIM, 1), lambda b, i: (0, 0)),
        ],
        out_specs=pl.BlockSpec((_A, tq), lambda b, i: (0, b * nq + i)),
        out_shape=jax.ShapeDtypeStruct((DIFF_HEADS * DIFF_V_DIM, T), BF16),
        scratch_shapes=[pltpu.VMEM((2 * DIFF_HEADS, DIFF_V_DIM, tq), F32)],
        compiler_params=_cparams(("arbitrary", "arbitrary")),
    )(zt, zn3, zt, lq1, lk1, lq2, lk2, subln)


def _sortable(x):
    b = pltpu.bitcast(x, jnp.int32)
    return b ^ ((b >> 31) & jnp.int32(0x7FFFFFFF))


def _dsa_body(qt_ref, qit_ref, wit_ref, k_ref, vt_ref, ki_ref, sl_ref, o_ref,
              keys_ref, acc_ref, l_ref, *, tq, tk, topk):
    i = pl.program_id(1)
    nkt = ((i + 1) * tq + tk - 1) // tk
    key0 = lax.broadcasted_iota(jnp.int32, (tk, tq), 0)
    qry = i * tq + lax.broadcasted_iota(jnp.int32, (tk, tq), 1)

    qit = qit_ref[...]
    w = wit_ref[...] * (IDX_HEADS ** -0.5 * IDX_DIM ** -0.5)

    def score_tile(c, carry):
        ki = ki_ref[pl.ds(pl.multiple_of(c * tk, tk), tk), :][:, :IDX_DIM]
        scs = []
        for r in range(tk // ATT_SUB):
            sc = jnp.zeros((ATT_SUB, tq), F32)
            for h in range(IDX_HEADS):
                d = _dot(ki[r * ATT_SUB:(r + 1) * ATT_SUB], qit[h * IDX_DIM:(h + 1) * IDX_DIM, :])
                sc = sc + jnp.maximum(d, 0.0) * w[h:h + 1, :]
            scs.append(sc)
        sc = jnp.concatenate(scs, axis=0)
        keys_ref[c] = jnp.where(c * tk + key0 <= qry, _sortable(sc), INT_MIN)
        return carry

    lax.fori_loop(0, nkt, score_tile, 0)

    def count_where(pred):
        def tile(c, acc):
            return acc + _fold_rows(jnp.where(pred(keys_ref[c]), 1.0, 0.0), jnp.add)
        acc = lax.fori_loop(0, nkt, tile, jnp.zeros((SUBLANE, tq), F32))
        return jnp.sum(acc, axis=0, keepdims=True)

    def pending(done):
        return (jnp.min(done) < 0.5).astype(jnp.int32)

    def search_cond(st):
        return jnp.logical_and(st[0] < 32, st[3] > 0)

    def search_body(st):
        p, t, done, _ = st
        cand = t ^ lax.shift_left(jnp.int32(1), 31 - p)
        cnt = count_where(lambda k: k >= cand)
        take = jnp.logical_and(cnt >= topk, done < 0.5)
        t = jnp.where(take, cand, t)
        done = jnp.where(jnp.logical_and(take, cnt == topk), 1.0, done)
        return p + 1, t, done, pending(done)

    qpos = i * tq + lax.broadcasted_iota(jnp.int32, (1, tq), 1)
    done0 = jnp.where(qpos < topk, 1.0, 0.0)
    _, thr, _, ties = lax.while_loop(
        search_cond, search_body,
        (jnp.int32(0), jnp.full((1, tq), INT_MIN, jnp.int32), done0, pending(done0)))
    thr = jnp.maximum(thr, INT_MIN + 1)

    qt = qt_ref[...]
    acc_ref[...] = jnp.zeros_like(acc_ref)

    def attend(c, bias, carry):
        off = pl.multiple_of(c * tk, tk)
        k = k_ref[pl.ds(off, tk), :]
        out = []
        for h in range(DSA_HEADS):
            hs = slice(h * DSA_HEAD_DIM, (h + 1) * DSA_HEAD_DIM)
            m_old, l_old = carry[2 * h], carry[2 * h + 1]
            vt = vt_ref[hs, pl.ds(off, tk)]
            ss = []
            for r in range(tk // ATT_SUB):
                rs = slice(r * ATT_SUB, (r + 1) * ATT_SUB)
                ss.append(_dot(k[rs, hs], qt[hs, :]) + bias[rs])
            mx = functools.reduce(jnp.maximum, [_fold_rows(s, jnp.maximum) for s in ss])
            m_new = jnp.maximum(m_old, jnp.max(mx, axis=0, keepdims=True))
            alpha = jnp.exp2(m_old - m_new)
            ps = [jnp.exp2(s - m_new) for s in ss]
            sm = functools.reduce(jnp.add, [_fold_rows(p, jnp.add) for p in ps])
            l_new = alpha * l_old + jnp.sum(sm, axis=0, keepdims=True)
            pv = functools.reduce(jnp.add, [
                _dot(vt[:, r * ATT_SUB:(r + 1) * ATT_SUB], p.astype(BF16)) for r, p in enumerate(ps)])
            acc_ref[h] = alpha * acc_ref[h] + pv
            out += [m_new, l_new]
        return tuple(out)

    init = (jnp.full((1, tq), NEG_BIG, F32), jnp.zeros((1, tq), F32)) * DSA_HEADS

    def finish(fin):
        for h in range(DSA_HEADS):
            l_ref[h] = jnp.broadcast_to(fin[2 * h + 1], (SUBLANE, tq))

    @pl.when(ties == 0)
    def _():
        def tile(c, carry):
            return attend(c, jnp.where(keys_ref[c] >= thr, 0.0, NEG_BIG), carry)
        finish(lax.fori_loop(0, nkt, tile, init))

    @pl.when(ties != 0)
    def _():
        need = topk - count_where(lambda k: k > thr)

        def tile(c, carry):
            key = keys_ref[c]
            eq = key == thr
            eqf = jnp.where(eq, 1.0, 0.0)
            rank = carry[0] + _dot(sl_ref[...], eqf.astype(BF16))
            tie_ok = jnp.where(eq, jnp.where(rank < need, 0.0, NEG_BIG), NEG_BIG)
            bias = jnp.where(key > thr, 0.0, tie_ok)
            return (carry[0] + _col_sum(eqf),) + attend(c, bias, carry[1:])

        finish(lax.fori_loop(0, nkt, tile, (jnp.zeros((1, tq), F32),) + init)[1:])

    for h in range(DSA_HEADS):
        hs = slice(h * DSA_HEAD_DIM, (h + 1) * DSA_HEAD_DIM)
        o_ref[hs, :] = (acc_ref[h] / l_ref[h][0:1, :]).astype(o_ref.dtype)


def _dsa(zt, wit, zn3, *, tq, tk):
    B, L, _ = zn3.shape
    T = B * L
    nq = L // tq
    topk = min(TOPK_MAX, L // 4)
    sl = (lax.broadcasted_iota(jnp.int32, (tk, tk), 1)
          < lax.broadcasted_iota(jnp.int32, (tk, tk), 0)).astype(BF16)
    once = pl.Buffered(1)
    return pl.pallas_call(
        functools.partial(_dsa_body, tq=tq, tk=tk, topk=topk),
        grid=(B, nq),
        in_specs=[
            pl.BlockSpec((_A, tq), lambda b, i: (0, b * nq + i)),
            pl.BlockSpec((_A, tq), lambda b, i: (2, b * nq + i)),
            pl.BlockSpec((IDX_HEADS, tq), lambda b, i: (0, b * nq + i)),
            pl.BlockSpec((None, L, _A), lambda b, i: (b, 0, 0), pipeline_mode=once),
            pl.BlockSpec((_A, L), lambda b, i: (1, b), pipeline_mode=once),
            pl.BlockSpec((None, L, LANE), lambda b, i: (b, 0, 2 * _A // LANE), pipeline_mode=once),
            pl.BlockSpec((tk, tk), lambda b, i: (0, 0), pipeline_mode=once),
        ],
        out_specs=pl.BlockSpec((_A, tq), lambda b, i: (0, b * nq + i)),
        out_shape=jax.ShapeDtypeStruct((_A, T), BF16),
        scratch_shapes=[pltpu.VMEM((L // tk, tk, tq), jnp.int32),
                        pltpu.VMEM((DSA_HEADS, DSA_HEAD_DIM, tq), F32),
                        pltpu.VMEM((DSA_HEADS, SUBLANE, tq), F32)],
        compiler_params=_cparams(("arbitrary", "arbitrary")),
    )(zt, zt, wit, zn3, zt, zn3, sl)


def _split_dot(m, x):
    hi = x.astype(BF16)
    lo = (x - hi.astype(F32)).astype(BF16)
    return _dot(m, hi) + _dot(m, lo)


def _rwkv_body(z_ref, mu_ref, w0_ref, wa_ref, a0_ref, gup_ref, kk_ref, ka_ref, rk_ref,
               lnw_ref, lnb_ref, seg_ref, tri_ref, o_ref, prev_ref, s_ref, *, C):
    c = pl.program_id(1)
    W = RWKV_WIDTH

    @pl.when(c == 0)
    def _():
        prev_ref[...] = jnp.zeros_like(prev_ref)
        s_ref[...] = jnp.zeros_like(s_ref)

    z = z_ref[...]
    rows = lax.broadcasted_iota(jnp.int32, z.shape, 0)
    zp = jnp.where(rows == 0, prev_ref[0:1, :], pltpu.roll(z, 1, 0))
    prev_ref[0:1, :] = z[C - 1:C, :]
    zs = z + (zp - z) * mu_ref[...]
    r = zs[:, 0:W]
    k = zs[:, W:2 * W]
    v = zs[:, 2 * W:3 * W]
    wa = zs[:, 3 * W:3 * W + LANE]
    gc = zs[:, 3 * W + LANE:3 * W + 2 * LANE]

    lane = lax.broadcasted_iota(jnp.int32, (C, LANE), 1)
    first = lane < RWKV_HEAD_DIM
    wa_t = jnp.where(first, jnp.tanh(wa), wa)
    lora = _dot(wa_t.astype(BF16), wa_ref[...])
    y = w0_ref[...] + lora[:, :W]
    w_log = -(jnp.maximum(-y, 0.0) + jnp.log(1.0 + jnp.exp(-jnp.abs(y)))) - 0.5
    lw = -jnp.exp(w_log)
    a = jax.nn.sigmoid(a0_ref[...] + lora[:, W:])
    g = _dot(jax.nn.sigmoid(gc).astype(BF16), gup_ref[...])
    seg = seg_ref[...]
    kk = k * kk_ref[...]
    kk = kk / jnp.maximum(jnp.sqrt(_dot((kk * kk).astype(BF16), seg)), 1e-12)
    k2 = k * (1.0 + (a - 1.0) * ka_ref[...])

    cum = _split_dot(tri_ref[...], lw)
    p_inv = jnp.exp(-cum)
    r_t = (r * jnp.exp(cum)).astype(BF16)
    k_t = (k2 * p_inv).astype(BF16)
    a_t = (-kk * jnp.exp(cum - lw)).astype(BF16)
    b_t = (kk * a * p_inv).astype(BF16)
    p_end = jnp.exp(cum[C - 1:C, :])
    vb = v.astype(BF16)

    zero = jnp.zeros((C, LANE), BF16)

    def stack(x):
        return jnp.concatenate([jnp.where(first, x, zero), jnp.where(first, zero, x)], axis=0)

    ri = lax.broadcasted_iota(jnp.int32, (2 * C, 2 * C), 0)
    ci = lax.broadcasted_iota(jnp.int32, (2 * C, 2 * C), 1)
    strict = ci < ri
    incl = ci <= ri
    eye = jnp.where(ci == ri, 1.0, 0.0)

    ys = []
    for p in range(RWKV_HEADS // 2):
        sl = slice(p * LANE, (p + 1) * LANE)
        v_st = stack(vb[:, sl])
        ar = jnp.concatenate([stack(a_t[:, sl]), stack(r_t[:, sl])], axis=0)
        bk = jnp.concatenate([stack(b_t[:, sl]), stack(k_t[:, sl])], axis=0)
        m = _dot_nt(ar, bk)
        l_ab = jnp.where(strict, m[:2 * C, :2 * C], 0.0)
        l_ak = jnp.where(strict, m[:2 * C, 2 * C:], 0.0)
        m_rb = jnp.where(incl, m[2 * C:, :2 * C], 0.0)
        m_rk = jnp.where(incl, m[2 * C:, 2 * C:], 0.0)
        t_inv = eye + l_ab
        pw = l_ab
        for _ in range(int(math.log2(C)) - 1):
            pwb = pw.astype(BF16)
            pw = _dot(pwb, pwb)
            t_inv = t_inv + _dot(t_inv.astype(BF16), pw.astype(BF16))
        s_old = s_ref[p]
        xs = _dot_nt(jnp.concatenate([a_t[:, sl], r_t[:, sl]], axis=0), s_old.astype(BF16))
        x_st = stack(xs[:C].astype(BF16)).astype(F32) + _dot(l_ak.astype(BF16), v_st)
        u_st = _dot(t_inv.astype(BF16), x_st.astype(BF16)).astype(BF16)
        uv = jnp.concatenate([u_st, v_st], axis=0)
        y_st = _dot(jnp.concatenate([m_rb, m_rk], axis=1).astype(BF16), uv)
        ys.append(xs[C:] + y_st[:C] + y_st[C:])
        s_ref[p] = (s_old + _dot_tn(uv, bk)) * p_end[:, sl]
    yy = jnp.concatenate(ys, axis=1)

    inv_n = 1.0 / RWKV_HEAD_DIM
    mean = _dot(yy.astype(BF16), seg) * inv_n
    d = yy - mean
    var = _dot((d * d).astype(BF16), seg) * inv_n
    yn = d * lax.rsqrt(var + RWKV_GN_EPS) * lnw_ref[...] + lnb_ref[...]
    bonus = _dot((r * k2 * rk_ref[...]).astype(BF16), seg) * v
    o_ref[...] = ((yn + bonus) * g).astype(o_ref.dtype)


def _rwkv(zr3, mu, w0, w_up, a0, a_up, g_up, k_k, k_a, r_k, ln_w, ln_b):
    B, L, _ = zr3.shape
    C = RWKV_CHUNK
    W = RWKV_WIDTH
    row = lambda t: t.reshape(1, -1)
    wa = jnp.zeros((LANE, 2 * W), F32)
    wa = wa.at[:DECAY_LORA, :W].set(w_up).at[DECAY_LORA:, W:].set(a_up).astype(BF16)
    hid = jnp.arange(W) // RWKV_HEAD_DIM
    seg = (hid[:, None] == hid[None, :]).astype(BF16)
    tri = (jnp.arange(C)[None, :] <= jnp.arange(C)[:, None]).astype(BF16)
    c2 = lambda b, c: (0, 0)
    vec = lambda n: pl.BlockSpec((1, n), c2)
    return pl.pallas_call(
        functools.partial(_rwkv_body, C=C),
        grid=(B, L // C),
        in_specs=[
            pl.BlockSpec((None, C, _RWKV_W), lambda b, c: (b, c, 0)),
            vec(_RWKV_W), vec(W),
            pl.BlockSpec((LANE, 2 * W), c2),
            vec(W),
            pl.BlockSpec((GATE_LORA, W), c2),
            vec(W), vec(W), vec(W), vec(W), vec(W),
            pl.BlockSpec((W, W), c2),
            pl.BlockSpec((C, C), c2),
        ],
        out_specs=pl.BlockSpec((None, C, W), lambda b, c: (b, c, 0)),
        out_shape=jax.ShapeDtypeStruct((B, L, W), BF16),
        scratch_shapes=[pltpu.VMEM((8, _RWKV_W), F32),
                        pltpu.VMEM((RWKV_HEADS // 2, LANE, LANE), F32)],
        compiler_params=_cparams(("arbitrary", "arbitrary")),
    )(zr3, row(mu), row(w0), wa, row(a0), g_up.astype(BF16), row(k_k), row(k_a), row(r_k),
      row(ln_w), row(ln_b), seg, tri)


def _merge_body(x_ref, yat_ref, yb_ref, yct_ref, g0_ref, g1_ref, g2_ref, wb_ref, wo_ref, gpost_ref, o_ref):
    m = (jax.nn.sigmoid(g0_ref[...]) * _dot_tn(yat_ref[...], wb_ref[0])
         + jax.nn.sigmoid(g1_ref[...]) * _dot(yb_ref[...], wb_ref[1])
         + jax.nn.sigmoid(g2_ref[...]) * _dot_tn(yct_ref[...], wb_ref[2]))
    y = _dot(m.astype(BF16), wo_ref[...])
    o_ref[...] = x_ref[...] + _rms(y, gpost_ref[...])


def _merge(x2, yat, yb, yct, zg, wb, wo, g_post, *, tm):
    T, D = x2.shape
    Wb = yb.shape[1]
    tok = lambda w, j: pl.BlockSpec((tm, w), lambda i: (i, j))
    feat = pl.BlockSpec((Wb, tm), lambda i: (0, i))
    return pl.pallas_call(
        _merge_body,
        grid=(T // tm,),
        in_specs=[
            tok(D, 0), feat, tok(Wb, 0), feat,
            tok(D, 0), tok(D, 1), tok(D, 2),
            pl.BlockSpec((N_BRANCH, Wb, D), lambda i: (0, 0, 0)),
            pl.BlockSpec((D, D), lambda i: (0, 0)),
            pl.BlockSpec((1, D), lambda i: (0, 0)),
        ],
        out_specs=tok(D, 0),
        out_shape=jax.ShapeDtypeStruct((T, D), F32),
        compiler_params=_cparams(("arbitrary",)),
    )(x2, yat, yb, yct, zg, zg, zg, wb, wo, g_post)


def _tiles(B, L):
    T = B * L
    return dict(
        tm_ffn=min(512, T), tf=256,
        tm_proj=min(1024, T),
        tm_merge=min(512, T),
        tq_dsa=min(256, L), tk_dsa=min(512, L),
        tq_diff=min(256, L), tk_diff=min(512, L),
    )


def _mixer(x2, B, L, l, cfg, mix_norm_pre, mix_norm_post, w_in, rwkv, diffp, w_branch, w_out):
    T, D = x2.shape
    g_pre = mix_norm_pre.reshape(1, D)
    wt, row_scale, wit_w, wn, wr, wg = _split_w_in(w_in)
    tm = cfg["tm_proj"]
    zt, wit = _proj_t(x2, g_pre, wt, row_scale, wit_w, tm=tm, tn=_A)
    zn = _proj(x2, g_pre, wn, BF16, tm=tm, tn=ZN_W)
    zr = _proj(x2, g_pre, wr, F32, tm=tm, tn=_RWKV_W // 2)
    zg = _proj(x2, g_pre, wg, F32, tm=tm, tn=D)
    zn3 = zn.reshape(B, L, ZN_W)

    yat = _dsa(zt, wit, zn3, tq=cfg["tq_dsa"], tk=cfg["tk_dsa"])
    yb = _rwkv(zr.reshape(B, L, _RWKV_W), *rwkv)
    lq1, lk1, lq2, lk2, subln = diffp
    lambda_init = 0.8 - 0.6 * math.exp(-0.3 * l)
    rowv = lambda t: t.reshape(1, -1)
    yct = _diff_attn(zt, zn3, rowv(lq1), rowv(lk1), rowv(lq2), rowv(lk2), subln.reshape(-1, 1),
                     lambda_init, tq=cfg["tq_diff"], tk=cfg["tk_diff"])

    return _merge(x2, yat, yb.reshape(T, -1), yct, zg,
                  w_branch.astype(BF16), w_out.astype(BF16), mix_norm_post.reshape(1, D),
                  tm=cfg["tm_merge"])


def kernel(x, ffn1_norm_pre, ffn1_norm_post, ffn1_w_gate, ffn1_w_up, ffn1_w_down,
           mix_norm_pre, mix_norm_post, w_in, rwkv_mu, rwkv_w0, rwkv_w_up, rwkv_a0, rwkv_a_up,
           rwkv_g_up, rwkv_k_k, rwkv_k_a, rwkv_r_k, rwkv_ln_w, rwkv_ln_b,
           diff_lambda_q1, diff_lambda_k1, diff_lambda_q2, diff_lambda_k2, diff_subln,
           w_branch, w_out, ffn2_norm_pre, ffn2_norm_post, ffn2_w_gate, ffn2_w_up, ffn2_w_down):
    B, L, D = x.shape
    depth = w_in.shape[0]
    cfg = _tiles(B, L)
    x2 = x.reshape(B * L, D)
    for l in range(depth):
        x2 = _ffn(x2, ffn1_norm_pre[l].reshape(1, D), ffn1_norm_post[l].reshape(1, D),
                  *_ffn_weights(ffn1_w_gate[l], ffn1_w_up[l], ffn1_w_down[l], cfg["tf"]),
                  tm=cfg["tm_ffn"])
        rwkv = (rwkv_mu[l], rwkv_w0[l], rwkv_w_up[l], rwkv_a0[l], rwkv_a_up[l], rwkv_g_up[l],
                rwkv_k_k[l], rwkv_k_a[l], rwkv_r_k[l], rwkv_ln_w[l], rwkv_ln_b[l])
        diffp = (diff_lambda_q1[l], diff_lambda_k1[l], diff_lambda_q2[l], diff_lambda_k2[l],
                 diff_subln[l])
        x2 = _mixer(x2, B, L, l, cfg, mix_norm_pre[l], mix_norm_post[l], w_in[l], rwkv, diffp,
                    w_branch[l], w_out[l])
        x2 = _ffn(x2, ffn2_norm_pre[l].reshape(1, D), ffn2_norm_post[l].reshape(1, D),
                  *_ffn_weights(ffn2_w_gate[l], ffn2_w_up[l], ffn2_w_down[l], cfg["tf"]),
                  tm=cfg["tm_ffn"])
    return x2.reshape(B, L, D)
```

```python
import functools
import math

import jax
import jax.numpy as jnp
from jax import lax
from jax.experimental import pallas as pl
from jax.experimental.pallas import tpu as pltpu

F32 = jnp.float32
BF16 = jnp.bfloat16

D_MODEL = 1024
D_FF = 2816
DSA_HEADS = 4
DSA_HEAD_DIM = 128
IDX_HEADS = 8
IDX_DIM = 64
TOPK_MAX = 256
RWKV_HEADS = 8
RWKV_HEAD_DIM = 64
RWKV_WIDTH = RWKV_HEADS * RWKV_HEAD_DIM
DECAY_LORA = 64
AAA_LORA = 64
GATE_LORA = 128
RWKV_GN_EPS = 64e-5
DIFF_HEADS = 4
DIFF_QK_DIM = 64
DIFF_V_DIM = 128
N_BRANCH = 3
BRANCH_WIDTH = 512
NORM_EPS = 1e-6

_A = DSA_HEADS * DSA_HEAD_DIM
_DSA_W = 3 * _A + IDX_HEADS * IDX_DIM + IDX_DIM + IDX_HEADS
_RWKV_W = 3 * RWKV_WIDTH + DECAY_LORA + AAA_LORA + GATE_LORA
_DIFF_W = 3 * 512
_GATE_W = N_BRANCH * D_MODEL

LANE = 128
SUBLANE = 8
VMEM_LIMIT = 56 * 1024 * 1024
NEG_BIG = -1e30
INT_MIN = -2 ** 31

ZT_ROWS = 5 * _A
ZN_W = 2 * _A + LANE

RWKV_CHUNK = 64
RWKV_BLOCK = 256
ATT_SUB = 128


def _cparams(sem):
    return pltpu.CompilerParams(dimension_semantics=sem, vmem_limit_bytes=VMEM_LIMIT)


def _rms(x, g):
    return x * lax.rsqrt(jnp.mean(x * x, axis=-1, keepdims=True) + NORM_EPS) * g


def _dot(a, b):
    return jnp.dot(a, b, preferred_element_type=F32)


def _dot_nt(a, b):
    return lax.dot_general(a, b, (((1,), (1,)), ((), ())), preferred_element_type=F32)


def _dot_tn(a, b):
    return lax.dot_general(a, b, (((0,), (0,)), ((), ())), preferred_element_type=F32)


def _ffn_body(x_ref, gpre_ref, gpost_ref, wg_ref, wu_ref, wd_ref, o_ref, h_ref, acc_ref, *, nf):
    x = x_ref[...]
    h_ref[...] = _rms(x, gpre_ref[...]).astype(BF16)
    acc_ref[...] = jnp.zeros_like(acc_ref)

    def chunk(c, carry):
        h = h_ref[...]
        g = _dot(h, wg_ref[c])
        u = _dot(h, wu_ref[c])
        a = (g * jax.nn.sigmoid(g) * u).astype(BF16)
        acc_ref[...] += _dot(a, wd_ref[c])
        return carry

    lax.fori_loop(0, nf, chunk, 0)
    o_ref[...] = x + 0.5 * _rms(acc_ref[...], gpost_ref[...])


def _ffn(x2, g_pre, g_post, wg, wu, wd, *, tm):
    T, D = x2.shape
    nf, _, tf = wg.shape
    const3 = lambda i: (0, 0, 0)
    return pl.pallas_call(
        functools.partial(_ffn_body, nf=nf),
        grid=(T // tm,),
        in_specs=[
            pl.BlockSpec((tm, D), lambda i: (i, 0)),
            pl.BlockSpec((1, D), lambda i: (0, 0)),
            pl.BlockSpec((1, D), lambda i: (0, 0)),
            pl.BlockSpec((nf, D, tf), const3, pipeline_mode=pl.Buffered(1)),
            pl.BlockSpec((nf, D, tf), const3, pipeline_mode=pl.Buffered(1)),
            pl.BlockSpec((nf, tf, D), const3, pipeline_mode=pl.Buffered(1)),
        ],
        out_specs=pl.BlockSpec((tm, D), lambda i: (i, 0)),
        out_shape=jax.ShapeDtypeStruct((T, D), F32),
        scratch_shapes=[pltpu.VMEM((tm, D), BF16), pltpu.VMEM((tm, D), F32)],
        compiler_params=_cparams(("arbitrary",)),
    )(x2, g_pre, g_post, wg, wu, wd)


def _ffn_weights(w_gate, w_up, w_down, tf):
    D, F = w_gate.shape
    nf = F // tf
    wg = w_gate.astype(BF16).reshape(D, nf, tf).transpose(1, 0, 2)
    wu = w_up.astype(BF16).reshape(D, nf, tf).transpose(1, 0, 2)
    wd = w_down.astype(BF16).reshape(nf, tf, D)
    return wg, wu, wd


def _proj_body(x_ref, g_ref, w_ref, o_ref, h_ref):
    @pl.when(pl.program_id(1) == 0)
    def _():
        h_ref[...] = _rms(x_ref[...], g_ref[...]).astype(BF16)

    o_ref[...] = _dot(h_ref[...], w_ref[...]).astype(o_ref.dtype)


def _proj(x2, g, w, out_dtype, *, tm, tn):
    T, D = x2.shape
    N = w.shape[1]
    return pl.pallas_call(
        _proj_body,
        grid=(T // tm, N // tn),
        in_specs=[
            pl.BlockSpec((tm, D), lambda i, j: (i, 0)),
            pl.BlockSpec((1, D), lambda i, j: (0, 0)),
            pl.BlockSpec((D, tn), lambda i, j: (0, j)),
        ],
        out_specs=pl.BlockSpec((tm, tn), lambda i, j: (i, j)),
        out_shape=jax.ShapeDtypeStruct((T, N), out_dtype),
        scratch_shapes=[pltpu.VMEM((tm, D), BF16)],
        compiler_params=_cparams(("arbitrary", "arbitrary")),
    )(x2, g, w)


def _proj_t_body(x_ref, g_ref, wt_ref, sc_ref, wit_ref, o_ref, wi_ref, h_ref):
    @pl.when(pl.program_id(1) == 0)
    def _():
        h = _rms(x_ref[...], g_ref[...]).astype(BF16)
        h_ref[...] = h
        wi_ref[...] = _dot_nt(wit_ref[...], h)

    o_ref[...] = (_dot_nt(wt_ref[...], h_ref[...]) * sc_ref[...]).astype(o_ref.dtype)


def _proj_t(x2, g, wt, row_scale, wit, *, tm, tn):
    T, D = x2.shape
    N = wt.shape[0]
    return pl.pallas_call(
        _proj_t_body,
        grid=(T // tm, N // tn),
        in_specs=[
            pl.BlockSpec((tm, D), lambda i, j: (i, 0)),
            pl.BlockSpec((1, D), lambda i, j: (0, 0)),
            pl.BlockSpec((tn, D), lambda i, j: (j, 0)),
            pl.BlockSpec((tn, 1), lambda i, j: (j, 0)),
            pl.BlockSpec((IDX_HEADS, D), lambda i, j: (0, 0)),
        ],
        out_specs=[pl.BlockSpec((tn, tm), lambda i, j: (j, i)),
                   pl.BlockSpec((IDX_HEADS, tm), lambda i, j: (0, i))],
        out_shape=[jax.ShapeDtypeStruct((N, T), BF16), jax.ShapeDtypeStruct((IDX_HEADS, T), F32)],
        scratch_shapes=[pltpu.VMEM((tm, D), BF16)],
        compiler_params=_cparams(("arbitrary", "arbitrary")),
    )(x2, g, wt, row_scale, wit)


def _split_w_in(w_in):
    D = w_in.shape[0]
    c = 0
    dsa = w_in[:, c:c + _DSA_W]; c += _DSA_W
    rw = w_in[:, c:c + _RWKV_W]; c += _RWKV_W
    df = w_in[:, c:c + _DIFF_W]; c += _DIFF_W
    gate = w_in[:, c:c + _GATE_W]
    q, k, v, qi = (dsa[:, j * _A:(j + 1) * _A] for j in range(4))
    ki = dsa[:, 4 * _A:4 * _A + IDX_DIM]
    wi = dsa[:, 4 * _A + IDX_DIM:]
    dq, dk, dv = (df[:, j * 512:(j + 1) * 512] for j in range(3))
    wt = jnp.concatenate([q, v, qi, dq, dv], axis=1).T.astype(BF16)
    ones = jnp.ones((_A,), F32)
    log2e = math.log2(math.e)
    row_scale = jnp.concatenate([ones * (DSA_HEAD_DIM ** -0.5 * log2e), ones, ones,
                                 ones * (DIFF_QK_DIM ** -0.5 * log2e), ones]).reshape(ZT_ROWS, 1)
    wn = jnp.concatenate([k, dk, ki, jnp.zeros((D, LANE - IDX_DIM), F32)], axis=1).astype(BF16)
    return wt, row_scale, wi.T.astype(BF16), wn, rw.astype(BF16), gate.astype(BF16)


def _fold_rows(x, op, group=SUBLANE):
    r, n = x.shape
    parts = x.reshape(r // group, group, n)
    while parts.shape[0] > 1:
        pairs = parts.reshape(parts.shape[0] // 2, 2, group, n)
        parts = op(pairs[:, 0], pairs[:, 1])
    return parts[0]


def _col_max(x):
    return jnp.max(_fold_rows(x, jnp.maximum), axis=0, keepdims=True)


def _col_sum(x):
    return jnp.sum(_fold_rows(x, jnp.add), axis=0, keepdims=True)


def _softmax_streams(scores, values_t, n, carry, acc_ref):
    out = []
    nxt = [piece() for piece in scores(0)]
    for j in range(n):
        ss = nxt
        later = scores(j + 1) if j + 1 < n else []
        m_old, l_old = carry[2 * j], carry[2 * j + 1]
        mx = functools.reduce(jnp.maximum, [_fold_rows(s, jnp.maximum) for s in ss])
        m_new = jnp.maximum(m_old, jnp.max(mx, axis=0, keepdims=True))
        alpha = jnp.exp2(m_old - m_new)
        ps, nxt = [], []
        for r, s in enumerate(ss):
            if r < len(later):
                nxt.append(later[r]())
            ps.append(jnp.exp2(s - m_new))
        sm = functools.reduce(jnp.add, [_fold_rows(p, jnp.add) for p in ps])
        l_new = alpha * l_old + jnp.sum(sm, axis=0, keepdims=True)
        vt = values_t(j)
        pv = functools.reduce(jnp.add, [
            _dot(vt[:, r * ATT_SUB:(r + 1) * ATT_SUB], p.astype(BF16)) for r, p in enumerate(ps)])
        acc_ref[j] = alpha * acc_ref[j] + pv
        out += [m_new, l_new]
    return tuple(out)


def _diff_body(qt_ref, k_ref, vt_ref, lq1_ref, lk1_ref, lq2_ref, lk2_ref, sub_ref, o_ref,
               acc_ref, *, tq, tk, lambda_init):
    i = pl.program_id(1)
    qt = qt_ref[...]
    first = (lax.broadcasted_iota(jnp.int32, qt.shape, 0) & DIFF_QK_DIM) == 0
    zero = jnp.zeros_like(qt)
    qts = (jnp.where(first, qt, zero), jnp.where(first, zero, qt))
    acc_ref[...] = jnp.zeros_like(acc_ref)
    n_full = (i * tq) // tk

    def step(c, carry, masked):
        off = pl.multiple_of(c * tk, tk)
        k = k_ref[pl.ds(off, tk), :]
        if masked:
            key = c * tk + lax.broadcasted_iota(jnp.int32, (tk, tq), 0)
            qry = i * tq + lax.broadcasted_iota(jnp.int32, (tk, tq), 1)
            keep = key <= qry
        hsl = lambda j: slice((j // 2) * LANE, (j // 2 + 1) * LANE)

        def piece(j, r):
            rs = slice(r * ATT_SUB, (r + 1) * ATT_SUB)
            s = _dot(k[rs, hsl(j)], qts[j % 2][hsl(j), :])
            return jnp.where(keep[rs], s, NEG_BIG) if masked else s

        def scores(j):
            return [functools.partial(piece, j, r) for r in range(tk // ATT_SUB)]

        return _softmax_streams(scores, lambda j: vt_ref[hsl(j), pl.ds(off, tk)], 2 * DIFF_HEADS,
                                carry, acc_ref)

    init = (jnp.full((1, tq), NEG_BIG, F32), jnp.zeros((1, tq), F32)) * (2 * DIFF_HEADS)
    carry = lax.fori_loop(0, n_full, lambda c, cr: step(c, cr, False), init)
    fin = step(n_full, carry, True)

    lam = (jnp.exp(jnp.sum(lq1_ref[...] * lk1_ref[...], axis=1, keepdims=True))
           - jnp.exp(jnp.sum(lq2_ref[...] * lk2_ref[...], axis=1, keepdims=True)) + lambda_init)
    gain = sub_ref[...] * (1.0 - lambda_init)
    for h in range(DIFF_HEADS):
        o = acc_ref[2 * h] / fin[4 * h + 1] - lam * (acc_ref[2 * h + 1] / fin[4 * h + 3])
        ms = jnp.sum(o * o, axis=0, keepdims=True) * (1.0 / DIFF_V_DIM)
        o_ref[h * LANE:(h + 1) * LANE, :] = (o * lax.rsqrt(ms + NORM_EPS) * gain).astype(o_ref.dtype)


def _diff_attn(zt, zn3, lq1, lk1, lq2, lk2, subln, lambda_init, *, tq, tk):
    B, L, _ = zn3.shape
    T = B * L
    nq = L // tq
    vec = lambda n: pl.BlockSpec((1, n), lambda b, i: (0, 0))
    once = pl.Buffered(1)
    return pl.pallas_call(
        functools.partial(_diff_body, tq=tq, tk=tk, lambda_init=lambda_init),
        grid=(B, nq),
        in_specs=[
            pl.BlockSpec((_A, tq), lambda b, i: (3, b * nq + i)),
            pl.BlockSpec((None, L, _A), lambda b, i: (b, 0, 1), pipeline_mode=once),
            pl.BlockSpec((_A, L), lambda b, i: (4, b), pipeline_mode=once),
            vec(DIFF_QK_DIM), vec(DIFF_QK_DIM), vec(DIFF_QK_DIM), vec(DIFF_QK_DIM),
            pl.BlockSpec((DIFF_V_DIM, 1), lambda b, i: (0, 0)),
        ],
        out_specs=pl.BlockSpec((_A, tq), lambda b, i: (0, b * nq + i)),
        out_shape=jax.ShapeDtypeStruct((DIFF_HEADS * DIFF_V_DIM, T), BF16),
        scratch_shapes=[pltpu.VMEM((2 * DIFF_HEADS, DIFF_V_DIM, tq), F32)],
        compiler_params=_cparams(("arbitrary", "arbitrary")),
    )(zt, zn3, zt, lq1, lk1, lq2, lk2, subln)


def _sortable(x):
    b = pltpu.bitcast(x, jnp.int32)
    return b ^ ((b >> 31) & jnp.int32(0x7FFFFFFF))


def _dsa_body(qt_ref, qit_ref, wit_ref, k_ref, vt_ref, ki_ref, sl_ref, o_ref,
              keys_ref, top_ref, acc_ref, l_ref, *, tq, tk, topk):
    i = pl.program_id(1)
    nkt = ((i + 1) * tq + tk - 1) // tk
    key0 = lax.broadcasted_iota(jnp.int32, (tk, tq), 0)
    qry = i * tq + lax.broadcasted_iota(jnp.int32, (tk, tq), 1)

    qit = qit_ref[...]
    w = wit_ref[...] * (IDX_HEADS ** -0.5 * IDX_DIM ** -0.5)

    def score_tile(c, carry):
        ki = ki_ref[pl.ds(pl.multiple_of(c * tk, tk), tk), :][:, :IDX_DIM]
        scs = []
        for r in range(tk // ATT_SUB):
            sc = jnp.zeros((ATT_SUB, tq), F32)
            for h in range(IDX_HEADS):
                d = _dot(ki[r * ATT_SUB:(r + 1) * ATT_SUB], qit[h * IDX_DIM:(h + 1) * IDX_DIM, :])
                sc = sc + jnp.maximum(d, 0.0) * w[h:h + 1, :]
            scs.append(sc)
        sc = jnp.concatenate(scs, axis=0)
        key = jnp.where(c * tk + key0 <= qry, _sortable(sc), INT_MIN)
        keys_ref[c] = key
        top_ref[c] = (key >> 16).astype(jnp.int16)
        return carry

    lax.fori_loop(0, nkt, score_tile, 0)

    def count_where(pred):
        def tile(c, acc):
            return acc + _fold_rows(jnp.where(pred(keys_ref[c]), 1.0, 0.0), jnp.add)
        acc = lax.fori_loop(0, nkt, tile, jnp.zeros((SUBLANE, tq), F32))
        return jnp.sum(acc, axis=0, keepdims=True)

    def count_top(cand):
        cand_t = (cand >> 16).astype(jnp.int16)
        one, zero = jnp.ones((), jnp.int16), jnp.zeros((), jnp.int16)

        def tile(c, acc):
            hit = jnp.where(top_ref[c] >= cand_t, one, zero)
            return acc + _fold_rows(hit, jnp.add, 2 * SUBLANE).astype(F32)
        acc = lax.fori_loop(0, nkt, tile, jnp.zeros((2 * SUBLANE, tq), F32))
        return jnp.sum(acc, axis=0, keepdims=True)

    def pending(done):
        return (jnp.min(done) < 0.5).astype(jnp.int32)

    def search(count, last):
        def cond(st):
            return jnp.logical_and(st[0] < last, st[3] > 0)

        def body(st):
            p, t, done, _ = st
            cand = t ^ lax.shift_left(jnp.int32(1), 31 - p)
            cnt = count(cand)
            take = jnp.logical_and(cnt >= topk, done < 0.5)
            t = jnp.where(take, cand, t)
            done = jnp.where(jnp.logical_and(take, cnt == topk), 1.0, done)
            return p + 1, t, done, pending(done)
        return cond, body

    qpos = i * tq + lax.broadcasted_iota(jnp.int32, (1, tq), 1)
    done0 = jnp.where(qpos < topk, 1.0, 0.0)
    st = (jnp.int32(0), jnp.full((1, tq), INT_MIN, jnp.int32), done0, pending(done0))
    st = lax.while_loop(*search(count_top, 16), st)
    st = lax.while_loop(*search(lambda cand: count_where(lambda k: k >= cand), 32), st)
    thr, ties = st[1], st[3]
    thr = jnp.maximum(thr, INT_MIN + 1)

    qt = qt_ref[...]
    acc_ref[...] = jnp.zeros_like(acc_ref)

    def attend(c, bias, carry):
        off = pl.multiple_of(c * tk, tk)
        k = k_ref[pl.ds(off, tk), :]
        hsl = lambda h: slice(h * DSA_HEAD_DIM, (h + 1) * DSA_HEAD_DIM)

        def piece(h, r):
            rs = slice(r * ATT_SUB, (r + 1) * ATT_SUB)
            return _dot(k[rs, hsl(h)], qt[hsl(h), :]) + bias[rs]

        def scores(h):
            return [functools.partial(piece, h, r) for r in range(tk // ATT_SUB)]

        return _softmax_streams(scores, lambda h: vt_ref[hsl(h), pl.ds(off, tk)], DSA_HEADS, carry, acc_ref)

    init = (jnp.full((1, tq), NEG_BIG, F32), jnp.zeros((1, tq), F32)) * DSA_HEADS

    def finish(fin):
        for h in range(DSA_HEADS):
            l_ref[h] = jnp.broadcast_to(fin[2 * h + 1], (SUBLANE, tq))

    @pl.when(ties == 0)
    def _():
        def tile(c, carry):
            return attend(c, jnp.where(keys_ref[c] >= thr, 0.0, NEG_BIG), carry)
        finish(lax.fori_loop(0, nkt, tile, init))

    @pl.when(ties != 0)
    def _():
        need = topk - count_where(lambda k: k > thr)

        def tile(c, carry):
            key = keys_ref[c]
            eq = key == thr
            eqf = jnp.where(eq, 1.0, 0.0)
            rank = carry[0] + _dot(sl_ref[...], eqf.astype(BF16))
            tie_ok = jnp.where(eq, jnp.where(rank < need, 0.0, NEG_BIG), NEG_BIG)
            bias = jnp.where(key > thr, 0.0, tie_ok)
            return (carry[0] + _col_sum(eqf),) + attend(c, bias, carry[1:])

        finish(lax.fori_loop(0, nkt, tile, (jnp.zeros((1, tq), F32),) + init)[1:])

    for h in range(DSA_HEADS):
        hs = slice(h * DSA_HEAD_DIM, (h + 1) * DSA_HEAD_DIM)
        o_ref[hs, :] = (acc_ref[h] / l_ref[h][0:1, :]).astype(o_ref.dtype)


def _dsa(zt, wit, zn3, *, tq, tk):
    B, L, _ = zn3.shape
    T = B * L
    nq = L // tq
    topk = min(TOPK_MAX, L // 4)
    sl = (lax.broadcasted_iota(jnp.int32, (tk, tk), 1)
          < lax.broadcasted_iota(jnp.int32, (tk, tk), 0)).astype(BF16)
    once = pl.Buffered(1)
    return pl.pallas_call(
        functools.partial(_dsa_body, tq=tq, tk=tk, topk=topk),
        grid=(B, nq),
        in_specs=[
            pl.BlockSpec((_A, tq), lambda b, i: (0, b * nq + i)),
            pl.BlockSpec((_A, tq), lambda b, i: (2, b * nq + i)),
            pl.BlockSpec((IDX_HEADS, tq), lambda b, i: (0, b * nq + i)),
            pl.BlockSpec((None, L, _A), lambda b, i: (b, 0, 0), pipeline_mode=once),
            pl.BlockSpec((_A, L), lambda b, i: (1, b), pipeline_mode=once),
            pl.BlockSpec((None, L, LANE), lambda b, i: (b, 0, 2 * _A // LANE), pipeline_mode=once),
            pl.BlockSpec((tk, tk), lambda b, i: (0, 0), pipeline_mode=once),
        ],
        out_specs=pl.BlockSpec((_A, tq), lambda b, i: (0, b * nq + i)),
        out_shape=jax.ShapeDtypeStruct((_A, T), BF16),
        scratch_shapes=[pltpu.VMEM((L // tk, tk, tq), jnp.int32),
                        pltpu.VMEM((L // tk, tk, tq), jnp.int16),
                        pltpu.VMEM((DSA_HEADS, DSA_HEAD_DIM, tq), F32),
                        pltpu.VMEM((DSA_HEADS, SUBLANE, tq), F32)],
        compiler_params=_cparams(("arbitrary", "arbitrary")),
    )(zt, zt, wit, zn3, zt, zn3, sl)


def _split_dot(m, x):
    hi = x.astype(BF16)
    lo = (x - hi.astype(F32)).astype(BF16)
    return _dot(m, hi) + _dot(m, lo)


def _rwkv_body(z_ref, mu_ref, w0_ref, wa_ref, a0_ref, gup_ref, kk_ref, ka_ref, rk_ref,
               lnw_ref, lnb_ref, seg_ref, tri_ref, o_ref, prev_ref, s_ref, *, C):
    c = pl.program_id(1)
    W = RWKV_WIDTH

    @pl.when(c == 0)
    def _():
        prev_ref[...] = jnp.zeros_like(prev_ref)
        s_ref[...] = jnp.zeros_like(s_ref)

    z = z_ref[...]
    R = z.shape[0]
    G = R // C
    rows = lax.broadcasted_iota(jnp.int32, z.shape, 0)
    zp = jnp.where(rows == 0, prev_ref[0:1, :], pltpu.roll(z, 1, 0))
    prev_ref[0:1, :] = z[R - 1:R, :]
    zs = z + (zp - z) * mu_ref[...]
    r = zs[:, 0:W]
    k = zs[:, W:2 * W]
    v = zs[:, 2 * W:3 * W]
    wa = zs[:, 3 * W:3 * W + LANE]
    gc = zs[:, 3 * W + LANE:3 * W + 2 * LANE]

    first_r = lax.broadcasted_iota(jnp.int32, (R, LANE), 1) < RWKV_HEAD_DIM
    first = lax.broadcasted_iota(jnp.int32, (C, LANE), 1) < RWKV_HEAD_DIM
    wa_t = jnp.where(first_r, jnp.tanh(wa), wa)
    lora = _dot(wa_t.astype(BF16), wa_ref[...])
    y = w0_ref[...] + lora[:, :W]
    w_log = -(jnp.maximum(-y, 0.0) + jnp.log(1.0 + jnp.exp(-jnp.abs(y)))) - 0.5
    lw = -jnp.exp(w_log)
    a = jax.nn.sigmoid(a0_ref[...] + lora[:, W:])
    gate = _dot(jax.nn.sigmoid(gc).astype(BF16), gup_ref[...])
    seg = seg_ref[...]
    kk = k * kk_ref[...]
    kk = kk / jnp.maximum(jnp.sqrt(_dot((kk * kk).astype(BF16), seg)), 1e-12)
    k2 = k * (1.0 + (a - 1.0) * ka_ref[...])

    cum = _split_dot(tri_ref[...], lw)
    p_inv = jnp.exp(-cum)
    r_t = (r * jnp.exp(cum)).astype(BF16)
    k_t = (k2 * p_inv).astype(BF16)
    a_t = (-kk * jnp.exp(cum - lw)).astype(BF16)
    b_t = (kk * a * p_inv).astype(BF16)
    vb = v.astype(BF16)

    zero = jnp.zeros((C, LANE), BF16)

    def stack(x):
        return jnp.concatenate([jnp.where(first, x, zero), jnp.where(first, zero, x)], axis=0)

    ri = lax.broadcasted_iota(jnp.int32, (2 * C, 2 * C), 0)
    ci = lax.broadcasted_iota(jnp.int32, (2 * C, 2 * C), 1)
    strict = ci < ri
    incl = ci <= ri
    eye = jnp.where(ci == ri, 1.0, 0.0)
    pairs = RWKV_HEADS // 2

    units = [(g, p) for g in range(G) for p in range(pairs)]
    rw = lambda g: slice(g * C, (g + 1) * C)
    ln = lambda p: slice(p * LANE, (p + 1) * LANE)
    v_st = {u: stack(vb[rw(u[0]), ln(u[1])]) for u in units}
    bk = {u: jnp.concatenate([stack(b_t[rw(u[0]), ln(u[1])]), stack(k_t[rw(u[0]), ln(u[1])])], axis=0)
          for u in units}
    m = {u: _dot_nt(jnp.concatenate([stack(a_t[rw(u[0]), ln(u[1])]), stack(r_t[rw(u[0]), ln(u[1])])],
                                    axis=0), bk[u]) for u in units}
    l_ab = {u: jnp.where(strict, m[u][:2 * C, :2 * C], 0.0) for u in units}
    lv = {u: _dot(jnp.where(strict, m[u][:2 * C, 2 * C:], 0.0).astype(BF16), v_st[u]) for u in units}
    m_rbk = {u: jnp.concatenate([jnp.where(incl, m[u][2 * C:, :2 * C], 0.0),
                                 jnp.where(incl, m[u][2 * C:, 2 * C:], 0.0)], axis=1).astype(BF16)
             for u in units}
    t_inv = {u: eye + l_ab[u] for u in units}
    pw = {u: l_ab[u].astype(BF16) for u in units}
    for _ in range(int(math.log2(C)) - 1):
        pw = {u: _dot(pw[u], pw[u]).astype(BF16) for u in units}
        t_inv = {u: t_inv[u] + _dot(t_inv[u].astype(BF16), pw[u]) for u in units}
    t_inv = {u: t_inv[u].astype(BF16) for u in units}

    state = [s_ref[p] for p in range(pairs)]
    yrows = []
    for g in range(G):
        p_end = jnp.exp(cum[(g + 1) * C - 1:(g + 1) * C, :])
        us = [(g, p) for p in range(pairs)]
        xs = [_dot_nt(jnp.concatenate([a_t[rw(g), ln(p)], r_t[rw(g), ln(p)]], axis=0),
                      state[p].astype(BF16)) for p in range(pairs)]
        x_st = [(stack(xs[p][:C].astype(BF16)).astype(F32) + lv[us[p]]).astype(BF16) for p in range(pairs)]
        uv = [jnp.concatenate([_dot(t_inv[us[p]], x_st[p]).astype(BF16), v_st[us[p]]], axis=0)
              for p in range(pairs)]
        y_st = [_dot(m_rbk[us[p]], uv[p]) for p in range(pairs)]
        state = [(state[p] + _dot_tn(uv[p], bk[us[p]])) * p_end[:, ln(p)] for p in range(pairs)]
        yrows.append(jnp.concatenate([xs[p][C:] + y_st[p][:C] + y_st[p][C:] for p in range(pairs)], axis=1))
    for p in range(pairs):
        s_ref[p] = state[p]
    yy = jnp.concatenate(yrows, axis=0)

    inv_n = 1.0 / RWKV_HEAD_DIM
    mean = _dot(yy.astype(BF16), seg) * inv_n
    d = yy - mean
    var = _dot((d * d).astype(BF16), seg) * inv_n
    yn = d * lax.rsqrt(var + RWKV_GN_EPS) * lnw_ref[...] + lnb_ref[...]
    bonus = _dot((r * k2 * rk_ref[...]).astype(BF16), seg) * v
    o_ref[...] = ((yn + bonus) * gate).astype(o_ref.dtype)


def _rwkv(zr3, mu, w0, w_up, a0, a_up, g_up, k_k, k_a, r_k, ln_w, ln_b):
    B, L, _ = zr3.shape
    C = RWKV_CHUNK
    W = RWKV_WIDTH
    row = lambda t: t.reshape(1, -1)
    wa = jnp.zeros((LANE, 2 * W), F32)
    wa = wa.at[:DECAY_LORA, :W].set(w_up).at[DECAY_LORA:, W:].set(a_up).astype(BF16)
    hid = jnp.arange(W) // RWKV_HEAD_DIM
    seg = (hid[:, None] == hid[None, :]).astype(BF16)
    R = min(RWKV_BLOCK, L)
    tok = jnp.arange(R)
    tri = ((tok[None, :] <= tok[:, None]) & (tok[None, :] // C == tok[:, None] // C)).astype(BF16)
    c2 = lambda b, c: (0, 0)
    vec = lambda n: pl.BlockSpec((1, n), c2)
    return pl.pallas_call(
        functools.partial(_rwkv_body, C=C),
        grid=(B, L // R),
        in_specs=[
            pl.BlockSpec((None, R, _RWKV_W), lambda b, c: (b, c, 0)),
            vec(_RWKV_W), vec(W),
            pl.BlockSpec((LANE, 2 * W), c2),
            vec(W),
            pl.BlockSpec((GATE_LORA, W), c2),
            vec(W), vec(W), vec(W), vec(W), vec(W),
            pl.BlockSpec((W, W), c2),
            pl.BlockSpec((R, R), c2),
        ],
        out_specs=pl.BlockSpec((None, R, W), lambda b, c: (b, c, 0)),
        out_shape=jax.ShapeDtypeStruct((B, L, W), BF16),
        scratch_shapes=[pltpu.VMEM((8, _RWKV_W), F32),
                        pltpu.VMEM((RWKV_HEADS // 2, LANE, LANE), F32)],
        compiler_params=_cparams(("arbitrary", "arbitrary")),
    )(zr3, row(mu), row(w0), wa, row(a0), g_up.astype(BF16), row(k_k), row(k_a), row(r_k),
      row(ln_w), row(ln_b), seg, tri)


def _merge_body(x_ref, yat_ref, yb_ref, yct_ref, g0_ref, g1_ref, g2_ref, wb_ref, wo_ref, gpost_ref, o_ref):
    m = (jax.nn.sigmoid(g0_ref[...]) * _dot_tn(yat_ref[...], wb_ref[0])
         + jax.nn.sigmoid(g1_ref[...]) * _dot(yb_ref[...], wb_ref[1])
         + jax.nn.sigmoid(g2_ref[...]) * _dot_tn(yct_ref[...], wb_ref[2]))
    y = _dot(m.astype(BF16), wo_ref[...])
    o_ref[...] = x_ref[...] + _rms(y, gpost_ref[...])


def _merge(x2, yat, yb, yct, zg, wb, wo, g_post, *, tm):
    T, D = x2.shape
    Wb = yb.shape[1]
    tok = lambda w, j: pl.BlockSpec((tm, w), lambda i: (i, j))
    feat = pl.BlockSpec((Wb, tm), lambda i: (0, i))
    return pl.pallas_call(
        _merge_body,
        grid=(T // tm,),
        in_specs=[
            tok(D, 0), feat, tok(Wb, 0), feat,
            tok(D, 0), tok(D, 1), tok(D, 2),
            pl.BlockSpec((N_BRANCH, Wb, D), lambda i: (0, 0, 0)),
            pl.BlockSpec((D, D), lambda i: (0, 0)),
            pl.BlockSpec((1, D), lambda i: (0, 0)),
        ],
        out_specs=tok(D, 0),
        out_shape=jax.ShapeDtypeStruct((T, D), F32),
        compiler_params=_cparams(("arbitrary",)),
    )(x2, yat, yb, yct, zg, zg, zg, wb, wo, g_post)


def _tiles(B, L):
    T = B * L
    return dict(
        tm_ffn=min(512, T), tf=256,
        tm_proj=min(1024, T),
        tm_merge=min(512, T),
        tq_dsa=min(256, L), tk_dsa=min(512, L),
        tq_diff=min(256, L), tk_diff=min(512, L),
    )


def _mixer(x2, B, L, l, cfg, mix_norm_pre, mix_norm_post, w_in, rwkv, diffp, w_branch, w_out):
    T, D = x2.shape
    g_pre = mix_norm_pre.reshape(1, D)
    wt, row_scale, wit_w, wn, wr, wg = _split_w_in(w_in)
    tm = cfg["tm_proj"]
    zt, wit = _proj_t(x2, g_pre, wt, row_scale, wit_w, tm=tm, tn=_A)
    zn = _proj(x2, g_pre, wn, BF16, tm=tm, tn=ZN_W)
    zr = _proj(x2, g_pre, wr, F32, tm=tm, tn=_RWKV_W // 2)
    zg = _proj(x2, g_pre, wg, F32, tm=tm, tn=D)
    zn3 = zn.reshape(B, L, ZN_W)

    yat = _dsa(zt, wit, zn3, tq=cfg["tq_dsa"], tk=cfg["tk_dsa"])
    yb = _rwkv(zr.reshape(B, L, _RWKV_W), *rwkv)
    lq1, lk1, lq2, lk2, subln = diffp
    lambda_init = 0.8 - 0.6 * math.exp(-0.3 * l)
    rowv = lambda t: t.reshape(1, -1)
    yct = _diff_attn(zt, zn3, rowv(lq1), rowv(lk1), rowv(lq2), rowv(lk2), subln.reshape(-1, 1),
                     lambda_init, tq=cfg["tq_diff"], tk=cfg["tk_diff"])

    return _merge(x2, yat, yb.reshape(T, -1), yct, zg,
                  w_branch.astype(BF16), w_out.astype(BF16), mix_norm_post.reshape(1, D),
                  tm=cfg["tm_merge"])


def kernel(x, ffn1_norm_pre, ffn1_norm_post, ffn1_w_gate, ffn1_w_up, ffn1_w_down,
           mix_norm_pre, mix_norm_post, w_in, rwkv_mu, rwkv_w0, rwkv_w_up, rwkv_a0, rwkv_a_up,
           rwkv_g_up, rwkv_k_k, rwkv_k_a, rwkv_r_k, rwkv_ln_w, rwkv_ln_b,
           diff_lambda_q1, diff_lambda_k1, diff_lambda_q2, diff_lambda_k2, diff_subln,
           w_branch, w_out, ffn2_norm_pre, ffn2_norm_post, ffn2_w_gate, ffn2_w_up, ffn2_w_down):
    B, L, D = x.shape
    depth = w_in.shape[0]
    cfg = _tiles(B, L)
    x2 = x.reshape(B * L, D)
    for l in range(depth):
        x2 = _ffn(x2, ffn1_norm_pre[l].reshape(1, D), ffn1_norm_post[l].reshape(1, D),
                  *_ffn_weights(ffn1_w_gate[l], ffn1_w_up[l], ffn1_w_down[l], cfg["tf"]),
                  tm=cfg["tm_ffn"])
        rwkv = (rwkv_mu[l], rwkv_w0[l], rwkv_w_up[l], rwkv_a0[l], rwkv_a_up[l], rwkv_g_up[l],
                rwkv_k_k[l], rwkv_k_a[l], rwkv_r_k[l], rwkv_ln_w[l], rwkv_ln_b[l])
        diffp = (diff_lambda_q1[l], diff_lambda_k1[l], diff_lambda_q2[l], diff_lambda_k2[l],
                 diff_subln[l])
        x2 = _mixer(x2, B, L, l, cfg, mix_norm_pre[l], mix_norm_post[l], w_in[l], rwkv, diffp,
                    w_branch[l], w_out[l])
        x2 = _ffn(x2, ffn2_norm_pre[l].reshape(1, D), ffn2_norm_post[l].reshape(1, D),
                  *_ffn_weights(ffn2_w_gate[l], ffn2_w_up[l], ffn2_w_down[l], cfg["tf"]),
                  tm=cfg["tm_ffn"])
    return x2.reshape(B, L, D)
```

```python
import functools
import math

import jax
import jax.numpy as jnp
from jax import lax
from jax.experimental import pallas as pl
from jax.experimental.pallas import tpu as pltpu

F32 = jnp.float32
BF16 = jnp.bfloat16

D_MODEL = 1024
D_FF = 2816
DSA_HEADS = 4
DSA_HEAD_DIM = 128
IDX_HEADS = 8
IDX_DIM = 64
TOPK_MAX = 256
RWKV_HEADS = 8
RWKV_HEAD_DIM = 64
RWKV_WIDTH = RWKV_HEADS * RWKV_HEAD_DIM
DECAY_LORA = 64
AAA_LORA = 64
GATE_LORA = 128
RWKV_GN_EPS = 64e-5
DIFF_HEADS = 4
DIFF_QK_DIM = 64
DIFF_V_DIM = 128
N_BRANCH = 3
BRANCH_WIDTH = 512
NORM_EPS = 1e-6

_A = DSA_HEADS * DSA_HEAD_DIM
_DSA_W = 3 * _A + IDX_HEADS * IDX_DIM + IDX_DIM + IDX_HEADS
_RWKV_W = 3 * RWKV_WIDTH + DECAY_LORA + AAA_LORA + GATE_LORA
_DIFF_W = 3 * 512
_GATE_W = N_BRANCH * D_MODEL

LANE = 128
SUBLANE = 8
VMEM_LIMIT = 56 * 1024 * 1024
NEG_BIG = -1e30
INT_MIN = -2 ** 31

ZT_ROWS = 5 * _A
ZN_W = 2 * _A + LANE

RWKV_CHUNK = 64
RWKV_BLOCK = 256
ATT_SUB = 128


def _cparams(sem):
    return pltpu.CompilerParams(dimension_semantics=sem, vmem_limit_bytes=VMEM_LIMIT)


def _rms(x, g):
    return x * lax.rsqrt(jnp.mean(x * x, axis=-1, keepdims=True) + NORM_EPS) * g


def _dot(a, b):
    return jnp.dot(a, b, preferred_element_type=F32)


def _dot_nt(a, b):
    return lax.dot_general(a, b, (((1,), (1,)), ((), ())), preferred_element_type=F32)


def _dot_tn(a, b):
    return lax.dot_general(a, b, (((0,), (0,)), ((), ())), preferred_element_type=F32)


def _ffn_body(x_ref, gpre_ref, gpost_ref, wg_ref, wu_ref, wd_ref, o_ref, h_ref, acc_ref, *, nf):
    x = x_ref[...]
    h_ref[...] = _rms(x, gpre_ref[...]).astype(BF16)
    acc_ref[...] = jnp.zeros_like(acc_ref)

    def chunk(c, carry):
        h = h_ref[...]
        g = _dot(h, wg_ref[c])
        u = _dot(h, wu_ref[c])
        a = (g * jax.nn.sigmoid(g) * u).astype(BF16)
        acc_ref[...] += _dot(a, wd_ref[c])
        return carry

    lax.fori_loop(0, nf, chunk, 0)
    o_ref[...] = x + 0.5 * _rms(acc_ref[...], gpost_ref[...])


def _ffn(x2, g_pre, g_post, wg, wu, wd, *, tm):
    T, D = x2.shape
    nf, _, tf = wg.shape
    const3 = lambda i: (0, 0, 0)
    return pl.pallas_call(
        functools.partial(_ffn_body, nf=nf),
        grid=(T // tm,),
        in_specs=[
            pl.BlockSpec((tm, D), lambda i: (i, 0)),
            pl.BlockSpec((1, D), lambda i: (0, 0)),
            pl.BlockSpec((1, D), lambda i: (0, 0)),
            pl.BlockSpec((nf, D, tf), const3, pipeline_mode=pl.Buffered(1)),
            pl.BlockSpec((nf, D, tf), const3, pipeline_mode=pl.Buffered(1)),
            pl.BlockSpec((nf, tf, D), const3, pipeline_mode=pl.Buffered(1)),
        ],
        out_specs=pl.BlockSpec((tm, D), lambda i: (i, 0)),
        out_shape=jax.ShapeDtypeStruct((T, D), F32),
        scratch_shapes=[pltpu.VMEM((tm, D), BF16), pltpu.VMEM((tm, D), F32)],
        compiler_params=_cparams(("arbitrary",)),
    )(x2, g_pre, g_post, wg, wu, wd)


def _ffn_weights(w_gate, w_up, w_down, tf):
    D, F = w_gate.shape
    nf = F // tf
    wg = w_gate.astype(BF16).reshape(D, nf, tf).transpose(1, 0, 2)
    wu = w_up.astype(BF16).reshape(D, nf, tf).transpose(1, 0, 2)
    wd = w_down.astype(BF16).reshape(nf, tf, D)
    return wg, wu, wd


def _proj_body(x_ref, g_ref, *refs):
    n = len(refs) // 2
    h = _rms(x_ref[...], g_ref[...]).astype(BF16)
    for w_ref, o_ref in zip(refs[:n], refs[n:]):
        o_ref[...] = _dot(h, w_ref[...]).astype(o_ref.dtype)


def _proj(x2, g, ws, out_dtypes, *, tm):
    T, D = x2.shape
    return pl.pallas_call(
        _proj_body,
        grid=(T // tm,),
        in_specs=[pl.BlockSpec((tm, D), lambda i: (i, 0)), pl.BlockSpec((1, D), lambda i: (0, 0))]
        + [pl.BlockSpec(w.shape, lambda i: (0, 0), pipeline_mode=pl.Buffered(1)) for w in ws],
        out_specs=[pl.BlockSpec((tm, w.shape[1]), lambda i: (i, 0)) for w in ws],
        out_shape=[jax.ShapeDtypeStruct((T, w.shape[1]), dt) for w, dt in zip(ws, out_dtypes)],
        compiler_params=_cparams(("arbitrary",)),
    )(x2, g, *ws)


def _proj_t_body(x_ref, g_ref, wt_ref, sc_ref, wit_ref, o_ref, wi_ref, h_ref):
    @pl.when(pl.program_id(1) == 0)
    def _():
        h = _rms(x_ref[...], g_ref[...]).astype(BF16)
        h_ref[...] = h
        wi_ref[...] = _dot_nt(wit_ref[...], h)

    o_ref[...] = (_dot_nt(wt_ref[...], h_ref[...]) * sc_ref[...]).astype(o_ref.dtype)


def _proj_t(x2, g, wt, row_scale, wit, *, tm, tn):
    T, D = x2.shape
    N = wt.shape[0]
    return pl.pallas_call(
        _proj_t_body,
        grid=(T // tm, N // tn),
        in_specs=[
            pl.BlockSpec((tm, D), lambda i, j: (i, 0)),
            pl.BlockSpec((1, D), lambda i, j: (0, 0)),
            pl.BlockSpec((tn, D), lambda i, j: (j, 0)),
            pl.BlockSpec((tn, 1), lambda i, j: (j, 0)),
            pl.BlockSpec((IDX_HEADS, D), lambda i, j: (0, 0)),
        ],
        out_specs=[pl.BlockSpec((tn, tm), lambda i, j: (j, i)),
                   pl.BlockSpec((IDX_HEADS, tm), lambda i, j: (0, i))],
        out_shape=[jax.ShapeDtypeStruct((N, T), BF16), jax.ShapeDtypeStruct((IDX_HEADS, T), F32)],
        scratch_shapes=[pltpu.VMEM((tm, D), BF16)],
        compiler_params=_cparams(("arbitrary", "arbitrary")),
    )(x2, g, wt, row_scale, wit)


def _split_w_in(w_in):
    D = w_in.shape[0]
    c = 0
    dsa = w_in[:, c:c + _DSA_W]; c += _DSA_W
    rw = w_in[:, c:c + _RWKV_W]; c += _RWKV_W
    df = w_in[:, c:c + _DIFF_W]; c += _DIFF_W
    gate = w_in[:, c:c + _GATE_W]
    q, k, v, qi = (dsa[:, j * _A:(j + 1) * _A] for j in range(4))
    ki = dsa[:, 4 * _A:4 * _A + IDX_DIM]
    wi = dsa[:, 4 * _A + IDX_DIM:]
    dq, dk, dv = (df[:, j * 512:(j + 1) * 512] for j in range(3))
    wt = jnp.concatenate([q, v, qi, dq, dv], axis=1).T.astype(BF16)
    ones = jnp.ones((_A,), F32)
    log2e = math.log2(math.e)
    row_scale = jnp.concatenate([ones * (DSA_HEAD_DIM ** -0.5 * log2e), ones, ones,
                                 ones * (DIFF_QK_DIM ** -0.5 * log2e), ones]).reshape(ZT_ROWS, 1)
    wn = jnp.concatenate([k, dk, ki, jnp.zeros((D, LANE - IDX_DIM), F32)], axis=1).astype(BF16)
    return wt, row_scale, wi.T.astype(BF16), wn, rw.astype(BF16), gate.astype(BF16)


def _fori_by_two(n, body, init):
    carry = lax.fori_loop(0, n // 2, lambda j, cr: body(2 * j + 1, body(2 * j, cr)), init)
    return lax.fori_loop(2 * (n // 2), n, body, carry)


def _fold_rows(x, op, group=SUBLANE):
    r, n = x.shape
    parts = x.reshape(r // group, group, n)
    while parts.shape[0] > 1:
        pairs = parts.reshape(parts.shape[0] // 2, 2, group, n)
        parts = op(pairs[:, 0], pairs[:, 1])
    return parts[0]


def _col_max(x):
    return jnp.max(_fold_rows(x, jnp.maximum), axis=0, keepdims=True)


def _col_sum(x):
    return jnp.sum(_fold_rows(x, jnp.add), axis=0, keepdims=True)


def _softmax_streams(scores, values_t, n, carry, acc_ref):
    out = []
    nxt = [piece() for piece in scores(0)]
    for j in range(n):
        ss = nxt
        later = scores(j + 1) if j + 1 < n else []
        m_old, l_old = carry[2 * j], carry[2 * j + 1]
        mx = functools.reduce(jnp.maximum, [_fold_rows(s, jnp.maximum) for s in ss])
        m_new = jnp.maximum(m_old, jnp.max(mx, axis=0, keepdims=True))
        alpha = jnp.exp2(m_old - m_new)
        ps, nxt = [], []
        for r, s in enumerate(ss):
            if r < len(later):
                nxt.append(later[r]())
            ps.append(jnp.exp2(s - m_new))
        sm = functools.reduce(jnp.add, [_fold_rows(p, jnp.add) for p in ps])
        l_new = alpha * l_old + jnp.sum(sm, axis=0, keepdims=True)
        vt = values_t(j)
        pv = functools.reduce(jnp.add, [
            _dot(vt[:, r * ATT_SUB:(r + 1) * ATT_SUB], p.astype(BF16)) for r, p in enumerate(ps)])
        acc_ref[j] = alpha * acc_ref[j] + pv
        out += [m_new, l_new]
    return tuple(out)


def _diff_body(qt_ref, k_ref, vt_ref, lq1_ref, lk1_ref, lq2_ref, lk2_ref, sub_ref, o_ref,
               acc_ref, *, tq, tk, lambda_init):
    i = pl.program_id(1)
    qt = qt_ref[...]
    first = (lax.broadcasted_iota(jnp.int32, qt.shape, 0) & DIFF_QK_DIM) == 0
    zero = jnp.zeros_like(qt)
    qts = (jnp.where(first, qt, zero), jnp.where(first, zero, qt))
    acc_ref[...] = jnp.zeros_like(acc_ref)
    n_full = (i * tq) // tk

    def step(c, carry, masked):
        off = pl.multiple_of(c * tk, tk)
        k = k_ref[pl.ds(off, tk), :]
        if masked:
            key = c * tk + lax.broadcasted_iota(jnp.int32, (tk, tq), 0)
            qry = i * tq + lax.broadcasted_iota(jnp.int32, (tk, tq), 1)
            keep = key <= qry
        hsl = lambda j: slice((j // 2) * LANE, (j // 2 + 1) * LANE)

        def piece(j, r):
            rs = slice(r * ATT_SUB, (r + 1) * ATT_SUB)
            s = _dot(k[rs, hsl(j)], qts[j % 2][hsl(j), :])
            return jnp.where(keep[rs], s, NEG_BIG) if masked else s

        def scores(j):
            return [functools.partial(piece, j, r) for r in range(tk // ATT_SUB)]

        return _softmax_streams(scores, lambda j: vt_ref[hsl(j), pl.ds(off, tk)], 2 * DIFF_HEADS,
                                carry, acc_ref)

    init = (jnp.full((1, tq), NEG_BIG, F32), jnp.zeros((1, tq), F32)) * (2 * DIFF_HEADS)
    carry = _fori_by_two(n_full, lambda c, cr: step(c, cr, False), init)
    fin = step(n_full, carry, True)

    lam = (jnp.exp(jnp.sum(lq1_ref[...] * lk1_ref[...], axis=1, keepdims=True))
           - jnp.exp(jnp.sum(lq2_ref[...] * lk2_ref[...], axis=1, keepdims=True)) + lambda_init)
    gain = sub_ref[...] * (1.0 - lambda_init)
    for h in range(DIFF_HEADS):
        o = acc_ref[2 * h] / fin[4 * h + 1] - lam * (acc_ref[2 * h + 1] / fin[4 * h + 3])
        ms = jnp.sum(o * o, axis=0, keepdims=True) * (1.0 / DIFF_V_DIM)
        o_ref[h * LANE:(h + 1) * LANE, :] = (o * lax.rsqrt(ms + NORM_EPS) * gain).astype(o_ref.dtype)


def _diff_attn(zt, zn3, lq1, lk1, lq2, lk2, subln, lambda_init, *, tq, tk):
    B, L, _ = zn3.shape
    T = B * L
    nq = L // tq
    vec = lambda n: pl.BlockSpec((1, n), lambda b, i: (0, 0))
    once = pl.Buffered(1)
    return pl.pallas_call(
        functools.partial(_diff_body, tq=tq, tk=tk, lambda_init=lambda_init),
        grid=(B, nq),
        in_specs=[
            pl.BlockSpec((_A, tq), lambda b, i: (3, b * nq + i)),
            pl.BlockSpec((None, L, _A), lambda b, i: (b, 0, 1), pipeline_mode=once),
            pl.BlockSpec((_A, L), lambda b, i: (4, b), pipeline_mode=once),
            vec(DIFF_QK_DIM), vec(DIFF_QK_DIM), vec(DIFF_QK_DIM), vec(DIFF_QK_DIM),
            pl.BlockSpec((DIFF_V_DIM, 1), lambda b, i: (0, 0)),
        ],
        out_specs=pl.BlockSpec((_A, tq), lambda b, i: (0, b * nq + i)),
        out_shape=jax.ShapeDtypeStruct((DIFF_HEADS * DIFF_V_DIM, T), BF16),
        scratch_shapes=[pltpu.VMEM((2 * DIFF_HEADS, DIFF_V_DIM, tq), F32)],
        compiler_params=_cparams(("arbitrary", "arbitrary")),
    )(zt, zn3, zt, lq1, lk1, lq2, lk2, subln)


def _sortable(x):
    b = pltpu.bitcast(x, jnp.int32)
    return b ^ ((b >> 31) & jnp.int32(0x7FFFFFFF))


def _dsa_body(qt_ref, qit_ref, wit_ref, k_ref, vt_ref, ki_ref, sl_ref, o_ref,
              keys_ref, top_ref, acc_ref, l_ref, *, tq, tk, topk):
    i = pl.program_id(1)
    nkt = ((i + 1) * tq + tk - 1) // tk
    key0 = lax.broadcasted_iota(jnp.int32, (tk, tq), 0)
    qry = i * tq + lax.broadcasted_iota(jnp.int32, (tk, tq), 1)

    qit = qit_ref[...]
    w = wit_ref[...] * (IDX_HEADS ** -0.5 * IDX_DIM ** -0.5)

    def score_tile(c, carry):
        ki = ki_ref[pl.ds(pl.multiple_of(c * tk, tk), tk), :][:, :IDX_DIM]
        scs = []
        for r in range(tk // ATT_SUB):
            sc = jnp.zeros((ATT_SUB, tq), F32)
            for h in range(IDX_HEADS):
                d = _dot(ki[r * ATT_SUB:(r + 1) * ATT_SUB], qit[h * IDX_DIM:(h + 1) * IDX_DIM, :])
                sc = sc + jnp.maximum(d, 0.0) * w[h:h + 1, :]
            scs.append(sc)
        sc = jnp.concatenate(scs, axis=0)
        key = jnp.where(c * tk + key0 <= qry, _sortable(sc), INT_MIN)
        keys_ref[c] = key
        top_ref[c] = (key >> 16).astype(jnp.int16)
        return carry

    _fori_by_two(nkt, score_tile, 0)

    def count_where(pred):
        def tile(c, acc):
            return acc + _fold_rows(jnp.where(pred(keys_ref[c]), 1.0, 0.0), jnp.add)
        acc = lax.fori_loop(0, nkt, tile, jnp.zeros((SUBLANE, tq), F32))
        return jnp.sum(acc, axis=0, keepdims=True)

    one16, zero16 = jnp.ones((), jnp.int16), jnp.zeros((), jnp.int16)

    def count_half(half):
        def tile(c, acc):
            hit = jnp.where(top_ref[c] >= half, one16, zero16)
            return acc + _fold_rows(hit, jnp.add, 2 * SUBLANE).astype(F32)
        acc = lax.fori_loop(0, nkt, tile, jnp.zeros((2 * SUBLANE, tq), F32))
        return jnp.sum(acc, axis=0, keepdims=True)

    def upper_pass(p, st):
        t, done = st
        cand = t ^ lax.shift_left(jnp.int32(1), 31 - p)
        cnt = count_half((cand >> 16).astype(jnp.int16))
        take = jnp.logical_and(cnt >= topk, done < 0.5)
        return jnp.where(take, cand, t), jnp.where(jnp.logical_and(take, cnt == topk), 1.0, done)

    qpos = i * tq + lax.broadcasted_iota(jnp.int32, (1, tq), 1)
    thr, done = lax.fori_loop(0, 16, upper_pass, (jnp.full((1, tq), INT_MIN, jnp.int32),
                                                  jnp.where(qpos < topk, 1.0, 0.0)))

    upper = (thr >> 16).astype(jnp.int16)
    low_min = jnp.int16(-32768)

    def lower_tile(c, acc):
        top = top_ref[c]
        low = ((keys_ref[c] & jnp.int32(0xFFFF)) - 32768).astype(jnp.int16)
        top_ref[c] = jnp.where(top == upper, low, low_min)
        return acc + _fold_rows(jnp.where(top > upper, one16, zero16), jnp.add, 2 * SUBLANE).astype(F32)
    above = jnp.sum(lax.fori_loop(0, nkt, lower_tile, jnp.zeros((2 * SUBLANE, tq), F32)),
                    axis=0, keepdims=True)

    def pending(done):
        return (jnp.min(done) < 0.5).astype(jnp.int32)

    def lower_cond(st):
        return jnp.logical_and(st[0] < 32, st[3] > 0)

    def lower_pass(st):
        p, t, done, _ = st
        cand = t ^ lax.shift_left(jnp.int32(1), 31 - p)
        cnt = above + count_half(((cand & jnp.int32(0xFFFF)) - 32768).astype(jnp.int16))
        take = jnp.logical_and(cnt >= topk, done < 0.5)
        t = jnp.where(take, cand, t)
        done = jnp.where(jnp.logical_and(take, cnt == topk), 1.0, done)
        return p + 1, t, done, pending(done)

    _, thr, _, ties = lax.while_loop(lower_cond, lower_pass, (jnp.int32(16), thr, done, pending(done)))
    thr = jnp.maximum(thr, INT_MIN + 1)

    qt = qt_ref[...]
    acc_ref[...] = jnp.zeros_like(acc_ref)

    def attend(c, bias, carry):
        off = pl.multiple_of(c * tk, tk)
        k = k_ref[pl.ds(off, tk), :]
        hsl = lambda h: slice(h * DSA_HEAD_DIM, (h + 1) * DSA_HEAD_DIM)

        def piece(h, r):
            rs = slice(r * ATT_SUB, (r + 1) * ATT_SUB)
            return _dot(k[rs, hsl(h)], qt[hsl(h), :]) + bias[rs]

        def scores(h):
            return [functools.partial(piece, h, r) for r in range(tk // ATT_SUB)]

        return _softmax_streams(scores, lambda h: vt_ref[hsl(h), pl.ds(off, tk)], DSA_HEADS, carry, acc_ref)

    init = (jnp.full((1, tq), NEG_BIG, F32), jnp.zeros((1, tq), F32)) * DSA_HEADS

    def finish(fin):
        for h in range(DSA_HEADS):
            l_ref[h] = jnp.broadcast_to(fin[2 * h + 1], (SUBLANE, tq))

    @pl.when(ties == 0)
    def _():
        def tile(c, carry):
            return attend(c, jnp.where(keys_ref[c] >= thr, 0.0, NEG_BIG), carry)
        finish(_fori_by_two(nkt, tile, init))

    @pl.when(ties != 0)
    def _():
        need = topk - count_where(lambda k: k > thr)

        def tile(c, carry):
            key = keys_ref[c]
            eq = key == thr
            eqf = jnp.where(eq, 1.0, 0.0)
            rank = carry[0] + _dot(sl_ref[...], eqf.astype(BF16))
            tie_ok = jnp.where(eq, jnp.where(rank < need, 0.0, NEG_BIG), NEG_BIG)
            bias = jnp.where(key > thr, 0.0, tie_ok)
            return (carry[0] + _col_sum(eqf),) + attend(c, bias, carry[1:])

        finish(lax.fori_loop(0, nkt, tile, (jnp.zeros((1, tq), F32),) + init)[1:])

    for h in range(DSA_HEADS):
        hs = slice(h * DSA_HEAD_DIM, (h + 1) * DSA_HEAD_DIM)
        o_ref[hs, :] = (acc_ref[h] / l_ref[h][0:1, :]).astype(o_ref.dtype)


def _dsa(zt, wit, zn3, *, tq, tk):
    B, L, _ = zn3.shape
    T = B * L
    nq = L // tq
    topk = min(TOPK_MAX, L // 4)
    sl = (lax.broadcasted_iota(jnp.int32, (tk, tk), 1)
          < lax.broadcasted_iota(jnp.int32, (tk, tk), 0)).astype(BF16)
    once = pl.Buffered(1)
    return pl.pallas_call(
        functools.partial(_dsa_body, tq=tq, tk=tk, topk=topk),
        grid=(B, nq),
        in_specs=[
            pl.BlockSpec((_A, tq), lambda b, i: (0, b * nq + i)),
            pl.BlockSpec((_A, tq), lambda b, i: (2, b * nq + i)),
            pl.BlockSpec((IDX_HEADS, tq), lambda b, i: (0, b * nq + i)),
            pl.BlockSpec((None, L, _A), lambda b, i: (b, 0, 0), pipeline_mode=once),
            pl.BlockSpec((_A, L), lambda b, i: (1, b), pipeline_mode=once),
            pl.BlockSpec((None, L, LANE), lambda b, i: (b, 0, 2 * _A // LANE), pipeline_mode=once),
            pl.BlockSpec((tk, tk), lambda b, i: (0, 0), pipeline_mode=once),
        ],
        out_specs=pl.BlockSpec((_A, tq), lambda b, i: (0, b * nq + i)),
        out_shape=jax.ShapeDtypeStruct((_A, T), BF16),
        scratch_shapes=[pltpu.VMEM((L // tk, tk, tq), jnp.int32),
                        pltpu.VMEM((L // tk, tk, tq), jnp.int16),
                        pltpu.VMEM((DSA_HEADS, DSA_HEAD_DIM, tq), F32),
                        pltpu.VMEM((DSA_HEADS, SUBLANE, tq), F32)],
        compiler_params=_cparams(("arbitrary", "arbitrary")),
    )(zt, zt, wit, zn3, zt, zn3, sl)


def _split_dot(m, x):
    hi = x.astype(BF16)
    lo = (x - hi.astype(F32)).astype(BF16)
    return _dot(m, hi) + _dot(m, lo)


def _rwkv_body(z_ref, mu_ref, w0_ref, wa_ref, a0_ref, gup_ref, kk_ref, ka_ref, rk_ref,
               lnw_ref, lnb_ref, seg_ref, tri_ref, o_ref, prev_ref, s_ref, *, C):
    c = pl.program_id(1)
    W = RWKV_WIDTH

    @pl.when(c == 0)
    def _():
        prev_ref[...] = jnp.zeros_like(prev_ref)
        s_ref[...] = jnp.zeros_like(s_ref)

    z = z_ref[...]
    R = z.shape[0]
    G = R // C
    rows = lax.broadcasted_iota(jnp.int32, z.shape, 0)
    zp = jnp.where(rows == 0, prev_ref[0:1, :], pltpu.roll(z, 1, 0))
    prev_ref[0:1, :] = z[R - 1:R, :]
    zs = z + (zp - z) * mu_ref[...]
    r = zs[:, 0:W]
    k = zs[:, W:2 * W]
    v = zs[:, 2 * W:3 * W]
    wa = zs[:, 3 * W:3 * W + LANE]
    gc = zs[:, 3 * W + LANE:3 * W + 2 * LANE]

    first_r = lax.broadcasted_iota(jnp.int32, (R, LANE), 1) < RWKV_HEAD_DIM
    first = lax.broadcasted_iota(jnp.int32, (C, LANE), 1) < RWKV_HEAD_DIM
    wa_t = jnp.where(first_r, jnp.tanh(wa), wa)
    lora = _dot(wa_t.astype(BF16), wa_ref[...])
    y = w0_ref[...] + lora[:, :W]
    w_log = -(jnp.maximum(-y, 0.0) + jnp.log(1.0 + jnp.exp(-jnp.abs(y)))) - 0.5
    lw = -jnp.exp(w_log)
    a = jax.nn.sigmoid(a0_ref[...] + lora[:, W:])
    gate = _dot(jax.nn.sigmoid(gc).astype(BF16), gup_ref[...])
    seg = seg_ref[...]
    kk = k * kk_ref[...]
    kk = kk / jnp.maximum(jnp.sqrt(_dot((kk * kk).astype(BF16), seg)), 1e-12)
    k2 = k * (1.0 + (a - 1.0) * ka_ref[...])

    cum = _split_dot(tri_ref[...], lw)
    p_inv = jnp.exp(-cum)
    r_t = (r * jnp.exp(cum)).astype(BF16)
    k_t = (k2 * p_inv).astype(BF16)
    a_t = (-kk * jnp.exp(cum - lw)).astype(BF16)
    b_t = (kk * a * p_inv).astype(BF16)
    vb = v.astype(BF16)

    zero = jnp.zeros((C, LANE), BF16)

    def stack(x):
        return jnp.concatenate([jnp.where(first, x, zero), jnp.where(first, zero, x)], axis=0)

    ri = lax.broadcasted_iota(jnp.int32, (2 * C, 2 * C), 0)
    ci = lax.broadcasted_iota(jnp.int32, (2 * C, 2 * C), 1)
    strict = ci < ri
    incl = ci <= ri
    eye = jnp.where(ci == ri, 1.0, 0.0)
    pairs = RWKV_HEADS // 2

    units = [(g, p) for g in range(G) for p in range(pairs)]
    rw = lambda g: slice(g * C, (g + 1) * C)
    ln = lambda p: slice(p * LANE, (p + 1) * LANE)
    v_st = {u: stack(vb[rw(u[0]), ln(u[1])]) for u in units}
    bk = {u: jnp.concatenate([stack(b_t[rw(u[0]), ln(u[1])]), stack(k_t[rw(u[0]), ln(u[1])])], axis=0)
          for u in units}
    m = {u: _dot_nt(jnp.concatenate([stack(a_t[rw(u[0]), ln(u[1])]), stack(r_t[rw(u[0]), ln(u[1])])],
                                    axis=0), bk[u]) for u in units}
    l_ab = {u: jnp.where(strict, m[u][:2 * C, :2 * C], 0.0) for u in units}
    lv = {u: _dot(jnp.where(strict, m[u][:2 * C, 2 * C:], 0.0).astype(BF16), v_st[u]) for u in units}
    m_rbk = {u: jnp.concatenate([jnp.where(incl, m[u][2 * C:, :2 * C], 0.0),
                                 jnp.where(incl, m[u][2 * C:, 2 * C:], 0.0)], axis=1).astype(BF16)
             for u in units}
    t_inv = {u: eye + l_ab[u] for u in units}
    pw = {u: l_ab[u].astype(BF16) for u in units}
    for _ in range(int(math.log2(C)) - 1):
        pw = {u: _dot(pw[u], pw[u]).astype(BF16) for u in units}
        t_inv = {u: t_inv[u] + _dot(t_inv[u].astype(BF16), pw[u]) for u in units}
    t_inv = {u: t_inv[u].astype(BF16) for u in units}

    state = [s_ref[p] for p in range(pairs)]
    yrows = []
    for g in range(G):
        p_end = jnp.exp(cum[(g + 1) * C - 1:(g + 1) * C, :])
        us = [(g, p) for p in range(pairs)]
        xs = [_dot_nt(jnp.concatenate([a_t[rw(g), ln(p)], r_t[rw(g), ln(p)]], axis=0),
                      state[p].astype(BF16)) for p in range(pairs)]
        x_st = [(stack(xs[p][:C].astype(BF16)).astype(F32) + lv[us[p]]).astype(BF16) for p in range(pairs)]
        uv = [jnp.concatenate([_dot(t_inv[us[p]], x_st[p]).astype(BF16), v_st[us[p]]], axis=0)
              for p in range(pairs)]
        y_st = [_dot(m_rbk[us[p]], uv[p]) for p in range(pairs)]
        state = [(state[p] + _dot_tn(uv[p], bk[us[p]])) * p_end[:, ln(p)] for p in range(pairs)]
        yrows.append(jnp.concatenate([xs[p][C:] + y_st[p][:C] + y_st[p][C:] for p in range(pairs)], axis=1))
    for p in range(pairs):
        s_ref[p] = state[p]
    yy = jnp.concatenate(yrows, axis=0)

    inv_n = 1.0 / RWKV_HEAD_DIM
    mean = _dot(yy.astype(BF16), seg) * inv_n
    d = yy - mean
    var = _dot((d * d).astype(BF16), seg) * inv_n
    yn = d * lax.rsqrt(var + RWKV_GN_EPS) * lnw_ref[...] + lnb_ref[...]
    bonus = _dot((r * k2 * rk_ref[...]).astype(BF16), seg) * v
    o_ref[...] = ((yn + bonus) * gate).astype(o_ref.dtype)


def _rwkv(zr3, mu, w0, w_up, a0, a_up, g_up, k_k, k_a, r_k, ln_w, ln_b):
    B, L, _ = zr3.shape
    C = RWKV_CHUNK
    W = RWKV_WIDTH
    row = lambda t: t.reshape(1, -1)
    wa = jnp.zeros((LANE, 2 * W), F32)
    wa = wa.at[:DECAY_LORA, :W].set(w_up).at[DECAY_LORA:, W:].set(a_up).astype(BF16)
    hid = jnp.arange(W) // RWKV_HEAD_DIM
    seg = (hid[:, None] == hid[None, :]).astype(BF16)
    R = min(RWKV_BLOCK, L)
    tok = jnp.arange(R)
    tri = ((tok[None, :] <= tok[:, None]) & (tok[None, :] // C == tok[:, None] // C)).astype(BF16)
    c2 = lambda b, c: (0, 0)
    vec = lambda n: pl.BlockSpec((1, n), c2)
    return pl.pallas_call(
        functools.partial(_rwkv_body, C=C),
        grid=(B, L // R),
        in_specs=[
            pl.BlockSpec((None, R, _RWKV_W), lambda b, c: (b, c, 0)),
            vec(_RWKV_W), vec(W),
            pl.BlockSpec((LANE, 2 * W), c2),
            vec(W),
            pl.BlockSpec((GATE_LORA, W), c2),
            vec(W), vec(W), vec(W), vec(W), vec(W),
            pl.BlockSpec((W, W), c2),
            pl.BlockSpec((R, R), c2),
        ],
        out_specs=pl.BlockSpec((None, R, W), lambda b, c: (b, c, 0)),
        out_shape=jax.ShapeDtypeStruct((B, L, W), BF16),
        scratch_shapes=[pltpu.VMEM((8, _RWKV_W), F32),
                        pltpu.VMEM((RWKV_HEADS // 2, LANE, LANE), F32)],
        compiler_params=_cparams(("arbitrary", "arbitrary")),
    )(zr3, row(mu), row(w0), wa, row(a0), g_up.astype(BF16), row(k_k), row(k_a), row(r_k),
      row(ln_w), row(ln_b), seg, tri)


def _merge_body(x_ref, yat_ref, yb_ref, yct_ref, g0_ref, g1_ref, g2_ref, wb_ref, wo_ref, gpost_ref, o_ref):
    m = (jax.nn.sigmoid(g0_ref[...].astype(F32)) * _dot_tn(yat_ref[...], wb_ref[0])
         + jax.nn.sigmoid(g1_ref[...].astype(F32)) * _dot(yb_ref[...], wb_ref[1])
         + jax.nn.sigmoid(g2_ref[...].astype(F32)) * _dot_tn(yct_ref[...], wb_ref[2]))
    y = _dot(m.astype(BF16), wo_ref[...])
    o_ref[...] = x_ref[...] + _rms(y, gpost_ref[...])


def _merge(x2, yat, yb, yct, zg, wb, wo, g_post, *, tm):
    T, D = x2.shape
    Wb = yb.shape[1]
    tok = lambda w, j: pl.BlockSpec((tm, w), lambda i: (i, j))
    feat = pl.BlockSpec((Wb, tm), lambda i: (0, i))
    return pl.pallas_call(
        _merge_body,
        grid=(T // tm,),
        in_specs=[
            tok(D, 0), feat, tok(Wb, 0), feat,
            tok(D, 0), tok(D, 1), tok(D, 2),
            pl.BlockSpec((N_BRANCH, Wb, D), lambda i: (0, 0, 0)),
            pl.BlockSpec((D, D), lambda i: (0, 0)),
            pl.BlockSpec((1, D), lambda i: (0, 0)),
        ],
        out_specs=tok(D, 0),
        out_shape=jax.ShapeDtypeStruct((T, D), F32),
        compiler_params=_cparams(("arbitrary",)),
    )(x2, yat, yb, yct, zg, zg, zg, wb, wo, g_post)


def _tiles(B, L):
    T = B * L
    return dict(
        tm_ffn=min(512, T), tf=256,
        tm_proj=min(1024, T),
        tm_merge=min(512, T),
        tq_dsa=min(256, L), tk_dsa=min(512, L),
        tq_diff=min(512, L), tk_diff=min(512, L),
    )


def _mixer(x2, B, L, l, cfg, mix_norm_pre, mix_norm_post, w_in, rwkv, diffp, w_branch, w_out):
    T, D = x2.shape
    g_pre = mix_norm_pre.reshape(1, D)
    wt, row_scale, wit_w, wn, wr, wg = _split_w_in(w_in)
    tm = cfg["tm_proj"]
    zt, wit = _proj_t(x2, g_pre, wt, row_scale, wit_w, tm=tm, tn=_A)
    zn, zr, zg = _proj(x2, g_pre, (wn, wr, wg), (BF16, F32, BF16), tm=cfg["tm_merge"])
    zn3 = zn.reshape(B, L, ZN_W)

    yat = _dsa(zt, wit, zn3, tq=cfg["tq_dsa"], tk=cfg["tk_dsa"])
    yb = _rwkv(zr.reshape(B, L, _RWKV_W), *rwkv)
    lq1, lk1, lq2, lk2, subln = diffp
    lambda_init = 0.8 - 0.6 * math.exp(-0.3 * l)
    rowv = lambda t: t.reshape(1, -1)
    yct = _diff_attn(zt, zn3, rowv(lq1), rowv(lk1), rowv(lq2), rowv(lk2), subln.reshape(-1, 1),
                     lambda_init, tq=cfg["tq_diff"], tk=cfg["tk_diff"])

    return _merge(x2, yat, yb.reshape(T, -1), yct, zg,
                  w_branch.astype(BF16), w_out.astype(BF16), mix_norm_post.reshape(1, D),
                  tm=cfg["tm_merge"])


def kernel(x, ffn1_norm_pre, ffn1_norm_post, ffn1_w_gate, ffn1_w_up, ffn1_w_down,
           mix_norm_pre, mix_norm_post, w_in, rwkv_mu, rwkv_w0, rwkv_w_up, rwkv_a0, rwkv_a_up,
           rwkv_g_up, rwkv_k_k, rwkv_k_a, rwkv_r_k, rwkv_ln_w, rwkv_ln_b,
           diff_lambda_q1, diff_lambda_k1, diff_lambda_q2, diff_lambda_k2, diff_subln,
           w_branch, w_out, ffn2_norm_pre, ffn2_norm_post, ffn2_w_gate, ffn2_w_up, ffn2_w_down):
    B, L, D = x.shape
    depth = w_in.shape[0]
    cfg = _tiles(B, L)
    x2 = x.reshape(B * L, D)
    for l in range(depth):
        x2 = _ffn(x2, ffn1_norm_pre[l].reshape(1, D), ffn1_norm_post[l].reshape(1, D),
                  *_ffn_weights(ffn1_w_gate[l], ffn1_w_up[l], ffn1_w_down[l], cfg["tf"]),
                  tm=cfg["tm_ffn"])
        rwkv = (rwkv_mu[l], rwkv_w0[l], rwkv_w_up[l], rwkv_a0[l], rwkv_a_up[l], rwkv_g_up[l],
                rwkv_k_k[l], rwkv_k_a[l], rwkv_r_k[l], rwkv_ln_w[l], rwkv_ln_b[l])
        diffp = (diff_lambda_q1[l], diff_lambda_k1[l], diff_lambda_q2[l], diff_lambda_k2[l],
                 diff_subln[l])
        x2 = _mixer(x2, B, L, l, cfg, mix_norm_pre[l], mix_norm_post[l], w_in[l], rwkv, diffp,
                    w_branch[l], w_out[l])
        x2 = _ffn(x2, ffn2_norm_pre[l].reshape(1, D), ffn2_norm_post[l].reshape(1, D),
                  *_ffn_weights(ffn2_w_gate[l], ffn2_w_up[l], ffn2_w_down[l], cfg["tf"]),
                  tm=cfg["tm_ffn"])
    return x2.reshape(B, L, D)
```

```python
import functools
import math

import jax
import jax.numpy as jnp
from jax import lax
from jax.experimental import pallas as pl
from jax.experimental.pallas import tpu as pltpu

F32 = jnp.float32
BF16 = jnp.bfloat16

D_MODEL = 1024
D_FF = 2816
DSA_HEADS = 4
DSA_HEAD_DIM = 128
IDX_HEADS = 8
IDX_DIM = 64
TOPK_MAX = 256
RWKV_HEADS = 8
RWKV_HEAD_DIM = 64
RWKV_WIDTH = RWKV_HEADS * RWKV_HEAD_DIM
DECAY_LORA = 64
AAA_LORA = 64
GATE_LORA = 128
RWKV_GN_EPS = 64e-5
DIFF_HEADS = 4
DIFF_QK_DIM = 64
DIFF_V_DIM = 128
N_BRANCH = 3
BRANCH_WIDTH = 512
NORM_EPS = 1e-6

_A = DSA_HEADS * DSA_HEAD_DIM
_DSA_W = 3 * _A + IDX_HEADS * IDX_DIM + IDX_DIM + IDX_HEADS
_RWKV_W = 3 * RWKV_WIDTH + DECAY_LORA + AAA_LORA + GATE_LORA
_DIFF_W = 3 * 512
_GATE_W = N_BRANCH * D_MODEL

LANE = 128
SUBLANE = 8
VMEM_LIMIT = 56 * 1024 * 1024
NEG_BIG = -1e30
INT_MIN = -2 ** 31

ZT_ROWS = 5 * _A
ZN_W = 2 * _A + LANE

RWKV_CHUNK = 64
RWKV_BLOCK = 256
ATT_SUB = 128


def _cparams(sem):
    return pltpu.CompilerParams(dimension_semantics=sem, vmem_limit_bytes=VMEM_LIMIT)


def _rms(x, g):
    return x * lax.rsqrt(jnp.mean(x * x, axis=-1, keepdims=True) + NORM_EPS) * g


def _dot(a, b):
    return jnp.dot(a, b, preferred_element_type=F32)


def _dot_nt(a, b):
    return lax.dot_general(a, b, (((1,), (1,)), ((), ())), preferred_element_type=F32)


def _dot_tn(a, b):
    return lax.dot_general(a, b, (((0,), (0,)), ((), ())), preferred_element_type=F32)


def _ffn_body(x_ref, gpre_ref, gpost_ref, wg_ref, wu_ref, wd_ref, o_ref, h_ref, acc_ref, *, nf):
    x = x_ref[...]
    h_ref[...] = _rms(x, gpre_ref[...]).astype(BF16)
    acc_ref[...] = jnp.zeros_like(acc_ref)

    def chunk(c, carry):
        h = h_ref[...]
        g = _dot(h, wg_ref[c])
        u = _dot(h, wu_ref[c])
        a = (g * jax.nn.sigmoid(g) * u).astype(BF16)
        acc_ref[...] += _dot(a, wd_ref[c])
        return carry

    lax.fori_loop(0, nf, chunk, 0)
    o_ref[...] = x + 0.5 * _rms(acc_ref[...], gpost_ref[...])


def _ffn(x2, g_pre, g_post, wg, wu, wd, *, tm):
    T, D = x2.shape
    nf, _, tf = wg.shape
    const3 = lambda i: (0, 0, 0)
    return pl.pallas_call(
        functools.partial(_ffn_body, nf=nf),
        grid=(T // tm,),
        in_specs=[
            pl.BlockSpec((tm, D), lambda i: (i, 0)),
            pl.BlockSpec((1, D), lambda i: (0, 0)),
            pl.BlockSpec((1, D), lambda i: (0, 0)),
            pl.BlockSpec((nf, D, tf), const3, pipeline_mode=pl.Buffered(1)),
            pl.BlockSpec((nf, D, tf), const3, pipeline_mode=pl.Buffered(1)),
            pl.BlockSpec((nf, tf, D), const3, pipeline_mode=pl.Buffered(1)),
        ],
        out_specs=pl.BlockSpec((tm, D), lambda i: (i, 0)),
        out_shape=jax.ShapeDtypeStruct((T, D), F32),
        scratch_shapes=[pltpu.VMEM((tm, D), BF16), pltpu.VMEM((tm, D), F32)],
        compiler_params=_cparams(("arbitrary",)),
    )(x2, g_pre, g_post, wg, wu, wd)


def _ffn_weights(w_gate, w_up, w_down, tf):
    D, F = w_gate.shape
    nf = F // tf
    wg = w_gate.astype(BF16).reshape(D, nf, tf).transpose(1, 0, 2)
    wu = w_up.astype(BF16).reshape(D, nf, tf).transpose(1, 0, 2)
    wd = w_down.astype(BF16).reshape(nf, tf, D)
    return wg, wu, wd


def _proj_body(x_ref, g_ref, *refs):
    n = len(refs) // 2
    h = _rms(x_ref[...], g_ref[...]).astype(BF16)
    for w_ref, o_ref in zip(refs[:n], refs[n:]):
        o_ref[...] = _dot(h, w_ref[...]).astype(o_ref.dtype)


def _proj(x2, g, ws, out_dtypes, *, tm):
    T, D = x2.shape
    return pl.pallas_call(
        _proj_body,
        grid=(T // tm,),
        in_specs=[pl.BlockSpec((tm, D), lambda i: (i, 0)), pl.BlockSpec((1, D), lambda i: (0, 0))]
        + [pl.BlockSpec(w.shape, lambda i: (0, 0), pipeline_mode=pl.Buffered(1)) for w in ws],
        out_specs=[pl.BlockSpec((tm, w.shape[1]), lambda i: (i, 0)) for w in ws],
        out_shape=[jax.ShapeDtypeStruct((T, w.shape[1]), dt) for w, dt in zip(ws, out_dtypes)],
        compiler_params=_cparams(("arbitrary",)),
    )(x2, g, *ws)


def _proj_t_body(x_ref, g_ref, wt_ref, sc_ref, wit_ref, o_ref, wi_ref, h_ref):
    @pl.when(pl.program_id(1) == 0)
    def _():
        h = _rms(x_ref[...], g_ref[...]).astype(BF16)
        h_ref[...] = h
        wi_ref[...] = _dot_nt(wit_ref[...], h)

    o_ref[...] = (_dot_nt(wt_ref[...], h_ref[...]) * sc_ref[...]).astype(o_ref.dtype)


def _proj_t(x2, g, wt, row_scale, wit, *, tm, tn):
    T, D = x2.shape
    N = wt.shape[0]
    return pl.pallas_call(
        _proj_t_body,
        grid=(T // tm, N // tn),
        in_specs=[
            pl.BlockSpec((tm, D), lambda i, j: (i, 0)),
            pl.BlockSpec((1, D), lambda i, j: (0, 0)),
            pl.BlockSpec((tn, D), lambda i, j: (j, 0)),
            pl.BlockSpec((tn, 1), lambda i, j: (j, 0)),
            pl.BlockSpec((IDX_HEADS, D), lambda i, j: (0, 0)),
        ],
        out_specs=[pl.BlockSpec((tn, tm), lambda i, j: (j, i)),
                   pl.BlockSpec((IDX_HEADS, tm), lambda i, j: (0, i))],
        out_shape=[jax.ShapeDtypeStruct((N, T), BF16), jax.ShapeDtypeStruct((IDX_HEADS, T), F32)],
        scratch_shapes=[pltpu.VMEM((tm, D), BF16)],
        compiler_params=_cparams(("arbitrary", "arbitrary")),
    )(x2, g, wt, row_scale, wit)


def _split_w_in(w_in):
    D = w_in.shape[0]
    c = 0
    dsa = w_in[:, c:c + _DSA_W]; c += _DSA_W
    rw = w_in[:, c:c + _RWKV_W]; c += _RWKV_W
    df = w_in[:, c:c + _DIFF_W]; c += _DIFF_W
    gate = w_in[:, c:c + _GATE_W]
    q, k, v, qi = (dsa[:, j * _A:(j + 1) * _A] for j in range(4))
    ki = dsa[:, 4 * _A:4 * _A + IDX_DIM]
    wi = dsa[:, 4 * _A + IDX_DIM:]
    dq, dk, dv = (df[:, j * 512:(j + 1) * 512] for j in range(3))
    wt = jnp.concatenate([q, v, qi, dq, dv], axis=1).T.astype(BF16)
    ones = jnp.ones((_A,), F32)
    log2e = math.log2(math.e)
    row_scale = jnp.concatenate([ones * (DSA_HEAD_DIM ** -0.5 * log2e), ones, ones,
                                 ones * (DIFF_QK_DIM ** -0.5 * log2e), ones]).reshape(ZT_ROWS, 1)
    wn = jnp.concatenate([k, dk, ki, jnp.zeros((D, LANE - IDX_DIM), F32)], axis=1).astype(BF16)
    return wt, row_scale, wi.T.astype(BF16), wn, rw.astype(BF16), gate.astype(BF16)


def _fori_by_two(n, body, init):
    carry = lax.fori_loop(0, n // 2, lambda j, cr: body(2 * j + 1, body(2 * j, cr)), init)
    return lax.fori_loop(2 * (n // 2), n, body, carry)


def _fold_rows(x, op, group=SUBLANE):
    r, n = x.shape
    parts = x.reshape(r // group, group, n)
    while parts.shape[0] > 1:
        pairs = parts.reshape(parts.shape[0] // 2, 2, group, n)
        parts = op(pairs[:, 0], pairs[:, 1])
    return parts[0]


def _col_max(x):
    return jnp.max(_fold_rows(x, jnp.maximum), axis=0, keepdims=True)


def _col_sum(x):
    return jnp.sum(_fold_rows(x, jnp.add), axis=0, keepdims=True)


def _softmax_streams(scores, values_t, n, carry, acc_ref):
    out = []
    nxt = [piece() for piece in scores(0)]
    for j in range(n):
        ss = nxt
        later = scores(j + 1) if j + 1 < n else []
        m_old, l_old = carry[2 * j], carry[2 * j + 1]
        mx = functools.reduce(jnp.maximum, [_fold_rows(s, jnp.maximum) for s in ss])
        m_new = jnp.maximum(m_old, jnp.max(mx, axis=0, keepdims=True))
        alpha = jnp.exp2(m_old - m_new)
        ps, nxt = [], []
        for r, s in enumerate(ss):
            if r < len(later):
                nxt.append(later[r]())
            ps.append(jnp.exp2(s - m_new))
        sm = functools.reduce(jnp.add, [_fold_rows(p, jnp.add) for p in ps])
        l_new = alpha * l_old + jnp.sum(sm, axis=0, keepdims=True)
        vt = values_t(j)
        pv = functools.reduce(jnp.add, [
            _dot(vt[:, r * ATT_SUB:(r + 1) * ATT_SUB], p.astype(BF16)) for r, p in enumerate(ps)])
        acc_ref[j] = alpha * acc_ref[j] + pv
        out += [m_new, l_new]
    return tuple(out)


def _diff_body(qt_ref, k_ref, vt_ref, lq1_ref, lk1_ref, lq2_ref, lk2_ref, sub_ref, o_ref,
               acc_ref, *, tq, tk, lambda_init):
    i = pl.program_id(1)
    qt = qt_ref[...]
    first = (lax.broadcasted_iota(jnp.int32, qt.shape, 0) & DIFF_QK_DIM) == 0
    zero = jnp.zeros_like(qt)
    qts = (jnp.where(first, qt, zero), jnp.where(first, zero, qt))
    acc_ref[...] = jnp.zeros_like(acc_ref)
    n_full = (i * tq) // tk

    def step(c, carry, masked):
        off = pl.multiple_of(c * tk, tk)
        k = k_ref[pl.ds(off, tk), :]
        if masked:
            key = c * tk + lax.broadcasted_iota(jnp.int32, (tk, tq), 0)
            qry = i * tq + lax.broadcasted_iota(jnp.int32, (tk, tq), 1)
            keep = key <= qry
        hsl = lambda j: slice((j // 2) * LANE, (j // 2 + 1) * LANE)

        def piece(j, r):
            rs = slice(r * ATT_SUB, (r + 1) * ATT_SUB)
            s = _dot(k[rs, hsl(j)], qts[j % 2][hsl(j), :])
            return jnp.where(keep[rs], s, NEG_BIG) if masked else s

        def scores(j):
            return [functools.partial(piece, j, r) for r in range(tk // ATT_SUB)]

        return _softmax_streams(scores, lambda j: vt_ref[hsl(j), pl.ds(off, tk)], 2 * DIFF_HEADS,
                                carry, acc_ref)

    init = (jnp.full((1, tq), NEG_BIG, F32), jnp.zeros((1, tq), F32)) * (2 * DIFF_HEADS)
    carry = _fori_by_two(n_full, lambda c, cr: step(c, cr, False), init)
    fin = step(n_full, carry, True)

    lam = (jnp.exp(jnp.sum(lq1_ref[...] * lk1_ref[...], axis=1, keepdims=True))
           - jnp.exp(jnp.sum(lq2_ref[...] * lk2_ref[...], axis=1, keepdims=True)) + lambda_init)
    gain = sub_ref[...] * (1.0 - lambda_init)
    for h in range(DIFF_HEADS):
        o = acc_ref[2 * h] / fin[4 * h + 1] - lam * (acc_ref[2 * h + 1] / fin[4 * h + 3])
        ms = jnp.sum(o * o, axis=0, keepdims=True) * (1.0 / DIFF_V_DIM)
        o_ref[h * LANE:(h + 1) * LANE, :] = (o * lax.rsqrt(ms + NORM_EPS) * gain).astype(o_ref.dtype)


def _diff_attn(zt, zn3, lq1, lk1, lq2, lk2, subln, lambda_init, *, tq, tk):
    B, L, _ = zn3.shape
    T = B * L
    nq = L // tq
    vec = lambda n: pl.BlockSpec((1, n), lambda b, i: (0, 0))
    once = pl.Buffered(1)
    return pl.pallas_call(
        functools.partial(_diff_body, tq=tq, tk=tk, lambda_init=lambda_init),
        grid=(B, nq),
        in_specs=[
            pl.BlockSpec((_A, tq), lambda b, i: (3, b * nq + i)),
            pl.BlockSpec((None, L, _A), lambda b, i: (b, 0, 1), pipeline_mode=once),
            pl.BlockSpec((_A, L), lambda b, i: (4, b), pipeline_mode=once),
            vec(DIFF_QK_DIM), vec(DIFF_QK_DIM), vec(DIFF_QK_DIM), vec(DIFF_QK_DIM),
            pl.BlockSpec((DIFF_V_DIM, 1), lambda b, i: (0, 0)),
        ],
        out_specs=pl.BlockSpec((_A, tq), lambda b, i: (0, b * nq + i)),
        out_shape=jax.ShapeDtypeStruct((DIFF_HEADS * DIFF_V_DIM, T), BF16),
        scratch_shapes=[pltpu.VMEM((2 * DIFF_HEADS, DIFF_V_DIM, tq), F32)],
        compiler_params=_cparams(("arbitrary", "arbitrary")),
    )(zt, zn3, zt, lq1, lk1, lq2, lk2, subln)


def _sortable(x):
    b = pltpu.bitcast(x, jnp.int32)
    return b ^ ((b >> 31) & jnp.int32(0x7FFFFFFF))


def _dsa_body(qt_ref, qit_ref, wit_ref, k_ref, vt_ref, ki_ref, sl_ref, o_ref,
              keys_ref, top_ref, acc_ref, l_ref, *, tq, tk, topk):
    i = pl.program_id(1)
    nkt = ((i + 1) * tq + tk - 1) // tk
    key0 = lax.broadcasted_iota(jnp.int32, (tk, tq), 0)
    qry = i * tq + lax.broadcasted_iota(jnp.int32, (tk, tq), 1)

    qit = qit_ref[...]
    w = wit_ref[...] * (IDX_HEADS ** -0.5 * IDX_DIM ** -0.5)

    def score_tile(c, carry):
        ki = ki_ref[pl.ds(pl.multiple_of(c * tk, tk), tk), :][:, :IDX_DIM]
        scs = []
        for r in range(tk // ATT_SUB):
            sc = jnp.zeros((ATT_SUB, tq), F32)
            for h in range(IDX_HEADS):
                d = _dot(ki[r * ATT_SUB:(r + 1) * ATT_SUB], qit[h * IDX_DIM:(h + 1) * IDX_DIM, :])
                sc = sc + jnp.maximum(d, 0.0) * w[h:h + 1, :]
            scs.append(sc)
        sc = jnp.concatenate(scs, axis=0)
        key = jnp.where(c * tk + key0 <= qry, _sortable(sc), INT_MIN)
        keys_ref[c] = key
        top_ref[c] = (key >> 16).astype(jnp.int16)
        return carry

    _fori_by_two(nkt, score_tile, 0)

    def count_where(pred):
        def tile(c, acc):
            return acc + _fold_rows(jnp.where(pred(keys_ref[c]), 1.0, 0.0), jnp.add)
        acc = lax.fori_loop(0, nkt, tile, jnp.zeros((SUBLANE, tq), F32))
        return jnp.sum(acc, axis=0, keepdims=True)

    one16, zero16 = jnp.ones((), jnp.int16), jnp.zeros((), jnp.int16)

    def count_half(half):
        def tile(c, acc):
            hit = jnp.where(top_ref[c] >= half, one16, zero16)
            return acc + _fold_rows(hit, jnp.add, 2 * SUBLANE).astype(F32)
        acc = _fori_by_two(nkt, tile, jnp.zeros((2 * SUBLANE, tq), F32))
        return jnp.sum(acc, axis=0, keepdims=True)

    def upper_pass(p, st):
        t, done = st
        cand = t ^ lax.shift_left(jnp.int32(1), 31 - p)
        cnt = count_half((cand >> 16).astype(jnp.int16))
        take = jnp.logical_and(cnt >= topk, done < 0.5)
        return jnp.where(take, cand, t), jnp.where(jnp.logical_and(take, cnt == topk), 1.0, done)

    qpos = i * tq + lax.broadcasted_iota(jnp.int32, (1, tq), 1)
    thr, done = lax.fori_loop(0, 16, upper_pass, (jnp.full((1, tq), INT_MIN, jnp.int32),
                                                  jnp.where(qpos < topk, 1.0, 0.0)))

    upper = (thr >> 16).astype(jnp.int16)
    low_min = jnp.int16(-32768)

    def lower_tile(c, acc):
        top = top_ref[c]
        low = ((keys_ref[c] & jnp.int32(0xFFFF)) - 32768).astype(jnp.int16)
        top_ref[c] = jnp.where(top == upper, low, low_min)
        return acc + _fold_rows(jnp.where(top > upper, one16, zero16), jnp.add, 2 * SUBLANE).astype(F32)
    above = jnp.sum(_fori_by_two(nkt, lower_tile, jnp.zeros((2 * SUBLANE, tq), F32)),
                    axis=0, keepdims=True)

    def pending(done):
        return (jnp.min(done) < 0.5).astype(jnp.int32)

    def lower_cond(st):
        return jnp.logical_and(st[0] < 32, st[3] > 0)

    def lower_pass(st):
        p, t, done, _ = st
        for b in range(2):
            cand = t ^ lax.shift_left(jnp.int32(1), 31 - p - b)
            cnt = above + count_half(((cand & jnp.int32(0xFFFF)) - 32768).astype(jnp.int16))
            take = jnp.logical_and(cnt >= topk, done < 0.5)
            t = jnp.where(take, cand, t)
            done = jnp.where(jnp.logical_and(take, cnt == topk), 1.0, done)
        return p + 2, t, done, pending(done)

    _, thr, _, ties = lax.while_loop(lower_cond, lower_pass, (jnp.int32(16), thr, done, pending(done)))
    thr = jnp.maximum(thr, INT_MIN + 1)

    qt = qt_ref[...]
    acc_ref[...] = jnp.zeros_like(acc_ref)

    def attend(c, bias, carry):
        off = pl.multiple_of(c * tk, tk)
        k = k_ref[pl.ds(off, tk), :]
        hsl = lambda h: slice(h * DSA_HEAD_DIM, (h + 1) * DSA_HEAD_DIM)

        def piece(h, r):
            rs = slice(r * ATT_SUB, (r + 1) * ATT_SUB)
            return _dot(k[rs, hsl(h)], qt[hsl(h), :]) + bias[rs]

        def scores(h):
            return [functools.partial(piece, h, r) for r in range(tk // ATT_SUB)]

        return _softmax_streams(scores, lambda h: vt_ref[hsl(h), pl.ds(off, tk)], DSA_HEADS, carry, acc_ref)

    init = (jnp.full((1, tq), NEG_BIG, F32), jnp.zeros((1, tq), F32)) * DSA_HEADS

    def finish(fin):
        for h in range(DSA_HEADS):
            l_ref[h] = jnp.broadcast_to(fin[2 * h + 1], (SUBLANE, tq))

    @pl.when(ties == 0)
    def _():
        def tile(c, carry):
            return attend(c, jnp.where(keys_ref[c] >= thr, 0.0, NEG_BIG), carry)
        finish(_fori_by_two(nkt, tile, init))

    @pl.when(ties != 0)
    def _():
        need = topk - count_where(lambda k: k > thr)

        def tile(c, carry):
            key = keys_ref[c]
            eq = key == thr
            eqf = jnp.where(eq, 1.0, 0.0)
            rank = carry[0] + _dot(sl_ref[...], eqf.astype(BF16))
            tie_ok = jnp.where(eq, jnp.where(rank < need, 0.0, NEG_BIG), NEG_BIG)
            bias = jnp.where(key > thr, 0.0, tie_ok)
            return (carry[0] + _col_sum(eqf),) + attend(c, bias, carry[1:])

        finish(lax.fori_loop(0, nkt, tile, (jnp.zeros((1, tq), F32),) + init)[1:])

    for h in range(DSA_HEADS):
        hs = slice(h * DSA_HEAD_DIM, (h + 1) * DSA_HEAD_DIM)
        o_ref[hs, :] = (acc_ref[h] / l_ref[h][0:1, :]).astype(o_ref.dtype)


def _dsa(zt, wit, zn3, *, tq, tk):
    B, L, _ = zn3.shape
    T = B * L
    nq = L // tq
    topk = min(TOPK_MAX, L // 4)
    sl = (lax.broadcasted_iota(jnp.int32, (tk, tk), 1)
          < lax.broadcasted_iota(jnp.int32, (tk, tk), 0)).astype(BF16)
    once = pl.Buffered(1)
    return pl.pallas_call(
        functools.partial(_dsa_body, tq=tq, tk=tk, topk=topk),
        grid=(B, nq),
        in_specs=[
            pl.BlockSpec((_A, tq), lambda b, i: (0, b * nq + i)),
            pl.BlockSpec((_A, tq), lambda b, i: (2, b * nq + i)),
            pl.BlockSpec((IDX_HEADS, tq), lambda b, i: (0, b * nq + i)),
            pl.BlockSpec((None, L, _A), lambda b, i: (b, 0, 0), pipeline_mode=once),
            pl.BlockSpec((_A, L), lambda b, i: (1, b), pipeline_mode=once),
            pl.BlockSpec((None, L, LANE), lambda b, i: (b, 0, 2 * _A // LANE), pipeline_mode=once),
            pl.BlockSpec((tk, tk), lambda b, i: (0, 0), pipeline_mode=once),
        ],
        out_specs=pl.BlockSpec((_A, tq), lambda b, i: (0, b * nq + i)),
        out_shape=jax.ShapeDtypeStruct((_A, T), BF16),
        scratch_shapes=[pltpu.VMEM((L // tk, tk, tq), jnp.int32),
                        pltpu.VMEM((L // tk, tk, tq), jnp.int16),
                        pltpu.VMEM((DSA_HEADS, DSA_HEAD_DIM, tq), F32),
                        pltpu.VMEM((DSA_HEADS, SUBLANE, tq), F32)],
        compiler_params=_cparams(("arbitrary", "arbitrary")),
    )(zt, zt, wit, zn3, zt, zn3, sl)


def _split_dot(m, x):
    hi = x.astype(BF16)
    lo = (x - hi.astype(F32)).astype(BF16)
    return _dot(m, hi) + _dot(m, lo)


def _rwkv_body(z_ref, mu_ref, w0_ref, wa_ref, a0_ref, gup_ref, kk_ref, ka_ref, rk_ref,
               lnw_ref, lnb_ref, seg_ref, tri_ref, o_ref, prev_ref, s_ref, *, C):
    c = pl.program_id(1)
    W = RWKV_WIDTH

    @pl.when(c == 0)
    def _():
        prev_ref[...] = jnp.zeros_like(prev_ref)
        s_ref[...] = jnp.zeros_like(s_ref)

    z = z_ref[...]
    R = z.shape[0]
    G = R // C
    rows = lax.broadcasted_iota(jnp.int32, z.shape, 0)
    zp = jnp.where(rows == 0, prev_ref[0:1, :], pltpu.roll(z, 1, 0))
    prev_ref[0:1, :] = z[R - 1:R, :]
    zs = z + (zp - z) * mu_ref[...]
    r = zs[:, 0:W]
    k = zs[:, W:2 * W]
    v = zs[:, 2 * W:3 * W]
    wa = zs[:, 3 * W:3 * W + LANE]
    gc = zs[:, 3 * W + LANE:3 * W + 2 * LANE]

    first_r = lax.broadcasted_iota(jnp.int32, (R, LANE), 1) < RWKV_HEAD_DIM
    first = lax.broadcasted_iota(jnp.int32, (C, LANE), 1) < RWKV_HEAD_DIM
    wa_t = jnp.where(first_r, jnp.tanh(wa), wa)
    lora = _dot(wa_t.astype(BF16), wa_ref[...])
    y = w0_ref[...] + lora[:, :W]
    w_log = -(jnp.maximum(-y, 0.0) + jnp.log(1.0 + jnp.exp(-jnp.abs(y)))) - 0.5
    lw = -jnp.exp(w_log)
    a = jax.nn.sigmoid(a0_ref[...] + lora[:, W:])
    gate = _dot(jax.nn.sigmoid(gc).astype(BF16), gup_ref[...])
    seg = seg_ref[...]
    kk = k * kk_ref[...]
    kk = kk / jnp.maximum(jnp.sqrt(_dot((kk * kk).astype(BF16), seg)), 1e-12)
    k2 = k * (1.0 + (a - 1.0) * ka_ref[...])

    cum = _split_dot(tri_ref[...], lw)
    p_inv = jnp.exp(-cum)
    r_t = (r * jnp.exp(cum)).astype(BF16)
    k_t = (k2 * p_inv).astype(BF16)
    a_t = (-kk * jnp.exp(cum - lw)).astype(BF16)
    b_t = (kk * a * p_inv).astype(BF16)
    vb = v.astype(BF16)

    zero = jnp.zeros((C, LANE), BF16)

    def stack(x):
        return jnp.concatenate([jnp.where(first, x, zero), jnp.where(first, zero, x)], axis=0)

    ri = lax.broadcasted_iota(jnp.int32, (2 * C, 2 * C), 0)
    ci = lax.broadcasted_iota(jnp.int32, (2 * C, 2 * C), 1)
    strict = ci < ri
    incl = ci <= ri
    eye = jnp.where(ci == ri, 1.0, 0.0)
    pairs = RWKV_HEADS // 2

    units = [(g, p) for g in range(G) for p in range(pairs)]
    rw = lambda g: slice(g * C, (g + 1) * C)
    ln = lambda p: slice(p * LANE, (p + 1) * LANE)
    v_st = {u: stack(vb[rw(u[0]), ln(u[1])]) for u in units}
    bk = {u: jnp.concatenate([stack(b_t[rw(u[0]), ln(u[1])]), stack(k_t[rw(u[0]), ln(u[1])])], axis=0)
          for u in units}
    m = {u: _dot_nt(jnp.concatenate([stack(a_t[rw(u[0]), ln(u[1])]), stack(r_t[rw(u[0]), ln(u[1])])],
                                    axis=0), bk[u]) for u in units}
    l_ab = {u: jnp.where(strict, m[u][:2 * C, :2 * C], 0.0) for u in units}
    lv = {u: _dot(jnp.where(strict, m[u][:2 * C, 2 * C:], 0.0).astype(BF16), v_st[u]) for u in units}
    m_rbk = {u: jnp.concatenate([jnp.where(incl, m[u][2 * C:, :2 * C], 0.0),
                                 jnp.where(incl, m[u][2 * C:, 2 * C:], 0.0)], axis=1).astype(BF16)
             for u in units}
    t_inv = {u: eye + l_ab[u] for u in units}
    pw = {u: l_ab[u].astype(BF16) for u in units}
    for _ in range(int(math.log2(C)) - 1):
        pw = {u: _dot(pw[u], pw[u]).astype(BF16) for u in units}
        t_inv = {u: t_inv[u] + _dot(t_inv[u].astype(BF16), pw[u]) for u in units}
    t_inv = {u: t_inv[u].astype(BF16) for u in units}

    state = [s_ref[p] for p in range(pairs)]
    yrows = []
    for g in range(G):
        p_end = jnp.exp(cum[(g + 1) * C - 1:(g + 1) * C, :])
        us = [(g, p) for p in range(pairs)]
        xs = [_dot_nt(jnp.concatenate([a_t[rw(g), ln(p)], r_t[rw(g), ln(p)]], axis=0),
                      state[p].astype(BF16)) for p in range(pairs)]
        x_st = [(stack(xs[p][:C].astype(BF16)).astype(F32) + lv[us[p]]).astype(BF16) for p in range(pairs)]
        uv = [jnp.concatenate([_dot(t_inv[us[p]], x_st[p]).astype(BF16), v_st[us[p]]], axis=0)
              for p in range(pairs)]
        y_st = [_dot(m_rbk[us[p]], uv[p]) for p in range(pairs)]
        state = [(state[p] + _dot_tn(uv[p], bk[us[p]])) * p_end[:, ln(p)] for p in range(pairs)]
        yrows.append(jnp.concatenate([xs[p][C:] + y_st[p][:C] + y_st[p][C:] for p in range(pairs)], axis=1))
    for p in range(pairs):
        s_ref[p] = state[p]
    yy = jnp.concatenate(yrows, axis=0)

    inv_n = 1.0 / RWKV_HEAD_DIM
    mean = _dot(yy.astype(BF16), seg) * inv_n
    d = yy - mean
    var = _dot((d * d).astype(BF16), seg) * inv_n
    yn = d * lax.rsqrt(var + RWKV_GN_EPS) * lnw_ref[...] + lnb_ref[...]
    bonus = _dot((r * k2 * rk_ref[...]).astype(BF16), seg) * v
    o_ref[...] = ((yn + bonus) * gate).astype(o_ref.dtype)


def _rwkv(zr3, mu, w0, w_up, a0, a_up, g_up, k_k, k_a, r_k, ln_w, ln_b):
    B, L, _ = zr3.shape
    C = RWKV_CHUNK
    W = RWKV_WIDTH
    row = lambda t: t.reshape(1, -1)
    wa = jnp.zeros((LANE, 2 * W), F32)
    wa = wa.at[:DECAY_LORA, :W].set(w_up).at[DECAY_LORA:, W:].set(a_up).astype(BF16)
    hid = jnp.arange(W) // RWKV_HEAD_DIM
    seg = (hid[:, None] == hid[None, :]).astype(BF16)
    R = min(RWKV_BLOCK, L)
    tok = jnp.arange(R)
    tri = ((tok[None, :] <= tok[:, None]) & (tok[None, :] // C == tok[:, None] // C)).astype(BF16)
    c2 = lambda b, c: (0, 0)
    vec = lambda n: pl.BlockSpec((1, n), c2)
    return pl.pallas_call(
        functools.partial(_rwkv_body, C=C),
        grid=(B, L // R),
        in_specs=[
            pl.BlockSpec((None, R, _RWKV_W), lambda b, c: (b, c, 0)),
            vec(_RWKV_W), vec(W),
            pl.BlockSpec((LANE, 2 * W), c2),
            vec(W),
            pl.BlockSpec((GATE_LORA, W), c2),
            vec(W), vec(W), vec(W), vec(W), vec(W),
            pl.BlockSpec((W, W), c2),
            pl.BlockSpec((R, R), c2),
        ],
        out_specs=pl.BlockSpec((None, R, W), lambda b, c: (b, c, 0)),
        out_shape=jax.ShapeDtypeStruct((B, L, W), BF16),
        scratch_shapes=[pltpu.VMEM((8, _RWKV_W), F32),
                        pltpu.VMEM((RWKV_HEADS // 2, LANE, LANE), F32)],
        compiler_params=_cparams(("arbitrary", "arbitrary")),
    )(zr3, row(mu), row(w0), wa, row(a0), g_up.astype(BF16), row(k_k), row(k_a), row(r_k),
      row(ln_w), row(ln_b), seg, tri)


def _merge_body(x_ref, yat_ref, yb_ref, yct_ref, g0_ref, g1_ref, g2_ref, wb_ref, wo_ref, gpost_ref, o_ref):
    m = (jax.nn.sigmoid(g0_ref[...].astype(F32)) * _dot_tn(yat_ref[...], wb_ref[0])
         + jax.nn.sigmoid(g1_ref[...].astype(F32)) * _dot(yb_ref[...], wb_ref[1])
         + jax.nn.sigmoid(g2_ref[...].astype(F32)) * _dot_tn(yct_ref[...], wb_ref[2]))
    y = _dot(m.astype(BF16), wo_ref[...])
    o_ref[...] = x_ref[...] + _rms(y, gpost_ref[...])


def _merge(x2, yat, yb, yct, zg, wb, wo, g_post, *, tm):
    T, D = x2.shape
    Wb = yb.shape[1]
    tok = lambda w, j: pl.BlockSpec((tm, w), lambda i: (i, j))
    feat = pl.BlockSpec((Wb, tm), lambda i: (0, i))
    return pl.pallas_call(
        _merge_body,
        grid=(T // tm,),
        in_specs=[
            tok(D, 0), feat, tok(Wb, 0), feat,
            tok(D, 0), tok(D, 1), tok(D, 2),
            pl.BlockSpec((N_BRANCH, Wb, D), lambda i: (0, 0, 0)),
            pl.BlockSpec((D, D), lambda i: (0, 0)),
            pl.BlockSpec((1, D), lambda i: (0, 0)),
        ],
        out_specs=tok(D, 0),
        out_shape=jax.ShapeDtypeStruct((T, D), F32),
        compiler_params=_cparams(("arbitrary",)),
    )(x2, yat, yb, yct, zg, zg, zg, wb, wo, g_post)


def _tiles(B, L):
    T = B * L
    return dict(
        tm_ffn=min(512, T), tf=1408,
        tm_proj=min(1024, T),
        tm_merge=min(512, T),
        tq_dsa=min(256, L), tk_dsa=min(512, L),
        tq_diff=min(512, L), tk_diff=min(512, L),
    )


def _mixer(x2, B, L, l, cfg, mix_norm_pre, mix_norm_post, w_in, rwkv, diffp, w_branch, w_out):
    T, D = x2.shape
    g_pre = mix_norm_pre.reshape(1, D)
    wt, row_scale, wit_w, wn, wr, wg = _split_w_in(w_in)
    tm = cfg["tm_proj"]
    zt, wit = _proj_t(x2, g_pre, wt, row_scale, wit_w, tm=tm, tn=_A)
    zn, zr, zg = _proj(x2, g_pre, (wn, wr, wg), (BF16, F32, BF16), tm=cfg["tm_merge"])
    zn3 = zn.reshape(B, L, ZN_W)

    yat = _dsa(zt, wit, zn3, tq=cfg["tq_dsa"], tk=cfg["tk_dsa"])
    yb = _rwkv(zr.reshape(B, L, _RWKV_W), *rwkv)
    lq1, lk1, lq2, lk2, subln = diffp
    lambda_init = 0.8 - 0.6 * math.exp(-0.3 * l)
    rowv = lambda t: t.reshape(1, -1)
    yct = _diff_attn(zt, zn3, rowv(lq1), rowv(lk1), rowv(lq2), rowv(lk2), subln.reshape(-1, 1),
                     lambda_init, tq=cfg["tq_diff"], tk=cfg["tk_diff"])

    return _merge(x2, yat, yb.reshape(T, -1), yct, zg,
                  w_branch.astype(BF16), w_out.astype(BF16), mix_norm_post.reshape(1, D),
                  tm=cfg["tm_merge"])


def kernel(x, ffn1_norm_pre, ffn1_norm_post, ffn1_w_gate, ffn1_w_up, ffn1_w_down,
           mix_norm_pre, mix_norm_post, w_in, rwkv_mu, rwkv_w0, rwkv_w_up, rwkv_a0, rwkv_a_up,
           rwkv_g_up, rwkv_k_k, rwkv_k_a, rwkv_r_k, rwkv_ln_w, rwkv_ln_b,
           diff_lambda_q1, diff_lambda_k1, diff_lambda_q2, diff_lambda_k2, diff_subln,
           w_branch, w_out, ffn2_norm_pre, ffn2_norm_post, ffn2_w_gate, ffn2_w_up, ffn2_w_down):
    B, L, D = x.shape
    depth = w_in.shape[0]
    cfg = _tiles(B, L)
    x2 = x.reshape(B * L, D)
    for l in range(depth):
        x2 = _ffn(x2, ffn1_norm_pre[l].reshape(1, D), ffn1_norm_post[l].reshape(1, D),
                  *_ffn_weights(ffn1_w_gate[l], ffn1_w_up[l], ffn1_w_down[l], cfg["tf"]),
                  tm=cfg["tm_ffn"])
        rwkv = (rwkv_mu[l], rwkv_w0[l], rwkv_w_up[l], rwkv_a0[l], rwkv_a_up[l], rwkv_g_up[l],
                rwkv_k_k[l], rwkv_k_a[l], rwkv_r_k[l], rwkv_ln_w[l], rwkv_ln_b[l])
        diffp = (diff_lambda_q1[l], diff_lambda_k1[l], diff_lambda_q2[l], diff_lambda_k2[l],
                 diff_subln[l])
        x2 = _mixer(x2, B, L, l, cfg, mix_norm_pre[l], mix_norm_post[l], w_in[l], rwkv, diffp,
                    w_branch[l], w_out[l])
        x2 = _ffn(x2, ffn2_norm_pre[l].reshape(1, D), ffn2_norm_post[l].reshape(1, D),
                  *_ffn_weights(ffn2_w_gate[l], ffn2_w_up[l], ffn2_w_down[l], cfg["tf"]),
                  tm=cfg["tm_ffn"])
    return x2.reshape(B, L, D)
```

```python
import functools
import math

import jax
import jax.numpy as jnp
from jax import lax
from jax.experimental import pallas as pl
from jax.experimental.pallas import tpu as pltpu

F32 = jnp.float32
BF16 = jnp.bfloat16

D_MODEL = 1024
D_FF = 2816
DSA_HEADS = 4
DSA_HEAD_DIM = 128
IDX_HEADS = 8
IDX_DIM = 64
TOPK_MAX = 256
RWKV_HEADS = 8
RWKV_HEAD_DIM = 64
RWKV_WIDTH = RWKV_HEADS * RWKV_HEAD_DIM
DECAY_LORA = 64
AAA_LORA = 64
GATE_LORA = 128
RWKV_GN_EPS = 64e-5
DIFF_HEADS = 4
DIFF_QK_DIM = 64
DIFF_V_DIM = 128
N_BRANCH = 3
BRANCH_WIDTH = 512
NORM_EPS = 1e-6

_A = DSA_HEADS * DSA_HEAD_DIM
_DSA_W = 3 * _A + IDX_HEADS * IDX_DIM + IDX_DIM + IDX_HEADS
_RWKV_W = 3 * RWKV_WIDTH + DECAY_LORA + AAA_LORA + GATE_LORA
_DIFF_W = 3 * 512
_GATE_W = N_BRANCH * D_MODEL

LANE = 128
SUBLANE = 8
VMEM_LIMIT = 56 * 1024 * 1024
NEG_BIG = -1e30
INT_MIN = -2 ** 31

ZT_ROWS = 5 * _A
ZN_W = 2 * _A + LANE

RWKV_CHUNK = 64
RWKV_BLOCK = 256
ATT_SUB = 128


def _cparams(sem):
    return pltpu.CompilerParams(dimension_semantics=sem, vmem_limit_bytes=VMEM_LIMIT)


def _rms(x, g):
    return x * lax.rsqrt(jnp.mean(x * x, axis=-1, keepdims=True) + NORM_EPS) * g


def _dot(a, b):
    return jnp.dot(a, b, preferred_element_type=F32)


def _dot_nt(a, b):
    return lax.dot_general(a, b, (((1,), (1,)), ((), ())), preferred_element_type=F32)


def _dot_tn(a, b):
    return lax.dot_general(a, b, (((0,), (0,)), ((), ())), preferred_element_type=F32)


def _ffn_body(x_ref, gpre_ref, gpost_ref, wg_ref, wu_ref, wd_ref, o_ref, h_ref, acc_ref, *, nf):
    x = x_ref[...]
    h_ref[...] = _rms(x, gpre_ref[...]).astype(BF16)
    acc_ref[...] = jnp.zeros_like(acc_ref)

    def chunk(c, carry):
        h = h_ref[...]
        g = _dot(h, wg_ref[c])
        u = _dot(h, wu_ref[c])
        a = (g * jax.nn.sigmoid(g) * u).astype(BF16)
        acc_ref[...] += _dot(a, wd_ref[c])
        return carry

    lax.fori_loop(0, nf, chunk, 0)
    o_ref[...] = x + 0.5 * _rms(acc_ref[...], gpost_ref[...])


def _ffn(x2, g_pre, g_post, wg, wu, wd, *, tm):
    T, D = x2.shape
    nf, _, tf = wg.shape
    const3 = lambda i: (0, 0, 0)
    return pl.pallas_call(
        functools.partial(_ffn_body, nf=nf),
        grid=(T // tm,),
        in_specs=[
            pl.BlockSpec((tm, D), lambda i: (i, 0)),
            pl.BlockSpec((1, D), lambda i: (0, 0)),
            pl.BlockSpec((1, D), lambda i: (0, 0)),
            pl.BlockSpec((nf, D, tf), const3, pipeline_mode=pl.Buffered(1)),
            pl.BlockSpec((nf, D, tf), const3, pipeline_mode=pl.Buffered(1)),
            pl.BlockSpec((nf, tf, D), const3, pipeline_mode=pl.Buffered(1)),
        ],
        out_specs=pl.BlockSpec((tm, D), lambda i: (i, 0)),
        out_shape=jax.ShapeDtypeStruct((T, D), F32),
        scratch_shapes=[pltpu.VMEM((tm, D), BF16), pltpu.VMEM((tm, D), F32)],
        compiler_params=_cparams(("arbitrary",)),
    )(x2, g_pre, g_post, wg, wu, wd)


def _ffn_weights(w_gate, w_up, w_down, tf):
    D, F = w_gate.shape
    nf = F // tf
    wg = w_gate.astype(BF16).reshape(D, nf, tf).transpose(1, 0, 2)
    wu = w_up.astype(BF16).reshape(D, nf, tf).transpose(1, 0, 2)
    wd = w_down.astype(BF16).reshape(nf, tf, D)
    return wg, wu, wd


def _proj_body(x_ref, g_ref, *refs):
    n = len(refs) // 2
    h = _rms(x_ref[...], g_ref[...]).astype(BF16)
    for w_ref, o_ref in zip(refs[:n], refs[n:]):
        o_ref[...] = _dot(h, w_ref[...]).astype(o_ref.dtype)


def _proj(x2, g, ws, out_dtypes, *, tm):
    T, D = x2.shape
    return pl.pallas_call(
        _proj_body,
        grid=(T // tm,),
        in_specs=[pl.BlockSpec((tm, D), lambda i: (i, 0)), pl.BlockSpec((1, D), lambda i: (0, 0))]
        + [pl.BlockSpec(w.shape, lambda i: (0, 0), pipeline_mode=pl.Buffered(1)) for w in ws],
        out_specs=[pl.BlockSpec((tm, w.shape[1]), lambda i: (i, 0)) for w in ws],
        out_shape=[jax.ShapeDtypeStruct((T, w.shape[1]), dt) for w, dt in zip(ws, out_dtypes)],
        compiler_params=_cparams(("arbitrary",)),
    )(x2, g, *ws)


def _proj_t_body(x_ref, g_ref, wt_ref, sc_ref, wit_ref, o_ref, wi_ref):
    h = _rms(x_ref[...], g_ref[...]).astype(BF16)
    wi_ref[...] = _dot_nt(wit_ref[...], h)
    o_ref[...] = (_dot_nt(wt_ref[...], h) * sc_ref[...]).astype(o_ref.dtype)


def _proj_t(x2, g, wt, row_scale, wit, *, tm):
    T, D = x2.shape
    N = wt.shape[0]
    once = pl.Buffered(1)
    return pl.pallas_call(
        _proj_t_body,
        grid=(T // tm,),
        in_specs=[
            pl.BlockSpec((tm, D), lambda i: (i, 0)),
            pl.BlockSpec((1, D), lambda i: (0, 0)),
            pl.BlockSpec((N, D), lambda i: (0, 0), pipeline_mode=once),
            pl.BlockSpec((N, 1), lambda i: (0, 0), pipeline_mode=once),
            pl.BlockSpec((IDX_HEADS, D), lambda i: (0, 0), pipeline_mode=once),
        ],
        out_specs=[pl.BlockSpec((N, tm), lambda i: (0, i)),
                   pl.BlockSpec((IDX_HEADS, tm), lambda i: (0, i))],
        out_shape=[jax.ShapeDtypeStruct((N, T), BF16), jax.ShapeDtypeStruct((IDX_HEADS, T), F32)],
        compiler_params=_cparams(("arbitrary",)),
    )(x2, g, wt, row_scale, wit)


def _split_w_in(w_in):
    D = w_in.shape[0]
    c = 0
    dsa = w_in[:, c:c + _DSA_W]; c += _DSA_W
    rw = w_in[:, c:c + _RWKV_W]; c += _RWKV_W
    df = w_in[:, c:c + _DIFF_W]; c += _DIFF_W
    gate = w_in[:, c:c + _GATE_W]
    q, k, v, qi = (dsa[:, j * _A:(j + 1) * _A] for j in range(4))
    ki = dsa[:, 4 * _A:4 * _A + IDX_DIM]
    wi = dsa[:, 4 * _A + IDX_DIM:]
    dq, dk, dv = (df[:, j * 512:(j + 1) * 512] for j in range(3))
    wt = jnp.concatenate([q, v, qi, dq, dv], axis=1).T.astype(BF16)
    ones = jnp.ones((_A,), F32)
    log2e = math.log2(math.e)
    row_scale = jnp.concatenate([ones * (DSA_HEAD_DIM ** -0.5 * log2e), ones, ones,
                                 ones * (DIFF_QK_DIM ** -0.5 * log2e), ones]).reshape(ZT_ROWS, 1)
    wn = jnp.concatenate([k, dk, ki, jnp.zeros((D, LANE - IDX_DIM), F32)], axis=1).astype(BF16)
    return wt, row_scale, wi.T.astype(BF16), wn, rw.astype(BF16), gate.astype(BF16)


def _fori_by_two(n, body, init):
    carry = lax.fori_loop(0, n // 2, lambda j, cr: body(2 * j + 1, body(2 * j, cr)), init)
    return lax.fori_loop(2 * (n // 2), n, body, carry)


def _fold_rows(x, op, group=SUBLANE):
    r, n = x.shape
    parts = x.reshape(r // group, group, n)
    while parts.shape[0] > 1:
        pairs = parts.reshape(parts.shape[0] // 2, 2, group, n)
        parts = op(pairs[:, 0], pairs[:, 1])
    return parts[0]


def _col_max(x):
    return jnp.max(_fold_rows(x, jnp.maximum), axis=0, keepdims=True)


def _col_sum(x):
    return jnp.sum(_fold_rows(x, jnp.add), axis=0, keepdims=True)


def _softmax_streams(scores, values_t, n, carry, acc_ref):
    out = []
    nxt = [piece() for piece in scores(0)]
    for j in range(n):
        ss = nxt
        later = scores(j + 1) if j + 1 < n else []
        m_old, l_old = carry[2 * j], carry[2 * j + 1]
        mx = functools.reduce(jnp.maximum, [_fold_rows(s, jnp.maximum) for s in ss])
        m_new = jnp.maximum(m_old, jnp.max(mx, axis=0, keepdims=True))
        alpha = jnp.exp2(m_old - m_new)
        ps, nxt = [], []
        for r, s in enumerate(ss):
            if r < len(later):
                nxt.append(later[r]())
            ps.append(jnp.exp2(s - m_new))
        sm = functools.reduce(jnp.add, [_fold_rows(p, jnp.add) for p in ps])
        l_new = alpha * l_old + jnp.sum(sm, axis=0, keepdims=True)
        vt = values_t(j)
        pv = functools.reduce(jnp.add, [
            _dot(vt[:, r * ATT_SUB:(r + 1) * ATT_SUB], p.astype(BF16)) for r, p in enumerate(ps)])
        acc_ref[j] = alpha * acc_ref[j] + pv
        out += [m_new, l_new]
    return tuple(out)


def _diff_body(qt_ref, k_ref, vt_ref, lq1_ref, lk1_ref, lq2_ref, lk2_ref, sub_ref, o_ref,
               acc_ref, *, tq, tk, lambda_init):
    i = pl.program_id(1)
    qt = qt_ref[...]
    first = (lax.broadcasted_iota(jnp.int32, qt.shape, 0) & DIFF_QK_DIM) == 0
    zero = jnp.zeros_like(qt)
    qts = (jnp.where(first, qt, zero), jnp.where(first, zero, qt))
    acc_ref[...] = jnp.zeros_like(acc_ref)
    n_full = (i * tq) // tk

    def step(c, carry, masked):
        off = pl.multiple_of(c * tk, tk)
        k = k_ref[pl.ds(off, tk), :]
        if masked:
            key = c * tk + lax.broadcasted_iota(jnp.int32, (tk, tq), 0)
            qry = i * tq + lax.broadcasted_iota(jnp.int32, (tk, tq), 1)
            keep = key <= qry
        hsl = lambda j: slice((j // 2) * LANE, (j // 2 + 1) * LANE)

        def piece(j, r):
            rs = slice(r * ATT_SUB, (r + 1) * ATT_SUB)
            s = _dot(k[rs, hsl(j)], qts[j % 2][hsl(j), :])
            return jnp.where(keep[rs], s, NEG_BIG) if masked else s

        def scores(j):
            return [functools.partial(piece, j, r) for r in range(tk // ATT_SUB)]

        return _softmax_streams(scores, lambda j: vt_ref[hsl(j), pl.ds(off, tk)], 2 * DIFF_HEADS,
                                carry, acc_ref)

    init = (jnp.full((1, tq), NEG_BIG, F32), jnp.zeros((1, tq), F32)) * (2 * DIFF_HEADS)
    carry = _fori_by_two(n_full, lambda c, cr: step(c, cr, False), init)
    fin = step(n_full, carry, True)

    lam = (jnp.exp(jnp.sum(lq1_ref[...] * lk1_ref[...], axis=1, keepdims=True))
           - jnp.exp(jnp.sum(lq2_ref[...] * lk2_ref[...], axis=1, keepdims=True)) + lambda_init)
    gain = sub_ref[...] * (1.0 - lambda_init)
    for h in range(DIFF_HEADS):
        o = acc_ref[2 * h] / fin[4 * h + 1] - lam * (acc_ref[2 * h + 1] / fin[4 * h + 3])
        ms = jnp.sum(o * o, axis=0, keepdims=True) * (1.0 / DIFF_V_DIM)
        o_ref[h * LANE:(h + 1) * LANE, :] = (o * lax.rsqrt(ms + NORM_EPS) * gain).astype(o_ref.dtype)


def _diff_attn(zt, zn3, lq1, lk1, lq2, lk2, subln, lambda_init, *, tq, tk):
    B, L, _ = zn3.shape
    T = B * L
    nq = L // tq
    vec = lambda n: pl.BlockSpec((1, n), lambda b, i: (0, 0))
    once = pl.Buffered(1)
    return pl.pallas_call(
        functools.partial(_diff_body, tq=tq, tk=tk, lambda_init=lambda_init),
        grid=(B, nq),
        in_specs=[
            pl.BlockSpec((_A, tq), lambda b, i: (3, b * nq + i)),
            pl.BlockSpec((None, L, _A), lambda b, i: (b, 0, 1), pipeline_mode=once),
            pl.BlockSpec((_A, L), lambda b, i: (4, b), pipeline_mode=once),
            vec(DIFF_QK_DIM), vec(DIFF_QK_DIM), vec(DIFF_QK_DIM), vec(DIFF_QK_DIM),
            pl.BlockSpec((DIFF_V_DIM, 1), lambda b, i: (0, 0)),
        ],
        out_specs=pl.BlockSpec((_A, tq), lambda b, i: (0, b * nq + i)),
        out_shape=jax.ShapeDtypeStruct((DIFF_HEADS * DIFF_V_DIM, T), BF16),
        scratch_shapes=[pltpu.VMEM((2 * DIFF_HEADS, DIFF_V_DIM, tq), F32)],
        compiler_params=_cparams(("arbitrary", "arbitrary")),
    )(zt, zn3, zt, lq1, lk1, lq2, lk2, subln)


def _sortable(x):
    b = pltpu.bitcast(x, jnp.int32)
    return b ^ ((b >> 31) & jnp.int32(0x7FFFFFFF))


def _dsa_body(qt_ref, qit_ref, wit_ref, k_ref, vt_ref, ki_ref, sl_ref, o_ref,
              keys_ref, top_ref, acc_ref, l_ref, *, tq, tk, topk):
    i = pl.program_id(1)
    nkt = ((i + 1) * tq + tk - 1) // tk
    key0 = lax.broadcasted_iota(jnp.int32, (tk, tq), 0)
    qry = i * tq + lax.broadcasted_iota(jnp.int32, (tk, tq), 1)

    qit = qit_ref[...]
    w = wit_ref[...] * (IDX_HEADS ** -0.5 * IDX_DIM ** -0.5)

    def score_tile(c, carry):
        ki = ki_ref[pl.ds(pl.multiple_of(c * tk, tk), tk), :][:, :IDX_DIM]
        scs = []
        for r in range(tk // ATT_SUB):
            sc = jnp.zeros((ATT_SUB, tq), F32)
            for h in range(IDX_HEADS):
                d = _dot(ki[r * ATT_SUB:(r + 1) * ATT_SUB], qit[h * IDX_DIM:(h + 1) * IDX_DIM, :])
                sc = sc + jnp.maximum(d, 0.0) * w[h:h + 1, :]
            scs.append(sc)
        sc = jnp.concatenate(scs, axis=0)
        key = jnp.where(c * tk + key0 <= qry, _sortable(sc), INT_MIN)
        keys_ref[c] = key
        top_ref[c] = (key >> 16).astype(jnp.int16)
        return carry

    _fori_by_two(nkt, score_tile, 0)

    def count_where(pred):
        def tile(c, acc):
            return acc + _fold_rows(jnp.where(pred(keys_ref[c]), 1.0, 0.0), jnp.add)
        acc = lax.fori_loop(0, nkt, tile, jnp.zeros((SUBLANE, tq), F32))
        return jnp.sum(acc, axis=0, keepdims=True)

    one16, zero16 = jnp.ones((), jnp.int16), jnp.zeros((), jnp.int16)

    def count_half(half):
        def tile(c, acc):
            hit = jnp.where(top_ref[c] >= half, one16, zero16)
            return acc + _fold_rows(hit, jnp.add, 2 * SUBLANE).astype(F32)
        acc = _fori_by_two(nkt, tile, jnp.zeros((2 * SUBLANE, tq), F32))
        return jnp.sum(acc, axis=0, keepdims=True)

    def upper_pass(p, st):
        t, done = st
        cand = t ^ lax.shift_left(jnp.int32(1), 31 - p)
        cnt = count_half((cand >> 16).astype(jnp.int16))
        take = jnp.logical_and(cnt >= topk, done < 0.5)
        return jnp.where(take, cand, t), jnp.where(jnp.logical_and(take, cnt == topk), 1.0, done)

    qpos = i * tq + lax.broadcasted_iota(jnp.int32, (1, tq), 1)
    thr, done = lax.fori_loop(0, 16, upper_pass, (jnp.full((1, tq), INT_MIN, jnp.int32),
                                                  jnp.where(qpos < topk, 1.0, 0.0)))

    upper = (thr >> 16).astype(jnp.int16)
    low_min = jnp.int16(-32768)

    def lower_tile(c, acc):
        top = top_ref[c]
        low = ((keys_ref[c] & jnp.int32(0xFFFF)) - 32768).astype(jnp.int16)
        top_ref[c] = jnp.where(top == upper, low, low_min)
        return acc + _fold_rows(jnp.where(top > upper, one16, zero16), jnp.add, 2 * SUBLANE).astype(F32)
    above = jnp.sum(_fori_by_two(nkt, lower_tile, jnp.zeros((2 * SUBLANE, tq), F32)),
                    axis=0, keepdims=True)

    def pending(done):
        return (jnp.min(done) < 0.5).astype(jnp.int32)

    def lower_cond(st):
        return jnp.logical_and(st[0] < 32, st[3] > 0)

    def lower_pass(st):
        p, t, done, _ = st
        for b in range(2):
            cand = t ^ lax.shift_left(jnp.int32(1), 31 - p - b)
            cnt = above + count_half(((cand & jnp.int32(0xFFFF)) - 32768).astype(jnp.int16))
            take = jnp.logical_and(cnt >= topk, done < 0.5)
            t = jnp.where(take, cand, t)
            done = jnp.where(jnp.logical_and(take, cnt == topk), 1.0, done)
        return p + 2, t, done, pending(done)

    _, thr, _, ties = lax.while_loop(lower_cond, lower_pass, (jnp.int32(16), thr, done, pending(done)))
    thr = jnp.maximum(thr, INT_MIN + 1)

    qt = qt_ref[...]
    acc_ref[...] = jnp.zeros_like(acc_ref)

    def attend(c, bias, carry):
        off = pl.multiple_of(c * tk, tk)
        k = k_ref[pl.ds(off, tk), :]
        hsl = lambda h: slice(h * DSA_HEAD_DIM, (h + 1) * DSA_HEAD_DIM)

        def piece(h, r):
            rs = slice(r * ATT_SUB, (r + 1) * ATT_SUB)
            return _dot(k[rs, hsl(h)], qt[hsl(h), :]) + bias[rs]

        def scores(h):
            return [functools.partial(piece, h, r) for r in range(tk // ATT_SUB)]

        return _softmax_streams(scores, lambda h: vt_ref[hsl(h), pl.ds(off, tk)], DSA_HEADS, carry, acc_ref)

    init = (jnp.full((1, tq), NEG_BIG, F32), jnp.zeros((1, tq), F32)) * DSA_HEADS

    def finish(fin):
        for h in range(DSA_HEADS):
            l_ref[h] = jnp.broadcast_to(fin[2 * h + 1], (SUBLANE, tq))

    @pl.when(ties == 0)
    def _():
        def tile(c, carry):
            return attend(c, jnp.where(keys_ref[c] >= thr, 0.0, NEG_BIG), carry)
        finish(_fori_by_two(nkt, tile, init))

    @pl.when(ties != 0)
    def _():
        need = topk - count_where(lambda k: k > thr)

        def tile(c, carry):
            key = keys_ref[c]
            eq = key == thr
            eqf = jnp.where(eq, 1.0, 0.0)
            rank = carry[0] + _dot(sl_ref[...], eqf.astype(BF16))
            tie_ok = jnp.where(eq, jnp.where(rank < need, 0.0, NEG_BIG), NEG_BIG)
            bias = jnp.where(key > thr, 0.0, tie_ok)
            return (carry[0] + _col_sum(eqf),) + attend(c, bias, carry[1:])

        finish(lax.fori_loop(0, nkt, tile, (jnp.zeros((1, tq), F32),) + init)[1:])

    for h in range(DSA_HEADS):
        hs = slice(h * DSA_HEAD_DIM, (h + 1) * DSA_HEAD_DIM)
        o_ref[hs, :] = (acc_ref[h] / l_ref[h][0:1, :]).astype(o_ref.dtype)


def _dsa(zt, wit, zn3, *, tq, tk):
    B, L, _ = zn3.shape
    T = B * L
    nq = L // tq
    topk = min(TOPK_MAX, L // 4)
    sl = (lax.broadcasted_iota(jnp.int32, (tk, tk), 1)
          < lax.broadcasted_iota(jnp.int32, (tk, tk), 0)).astype(BF16)
    once = pl.Buffered(1)
    return pl.pallas_call(
        functools.partial(_dsa_body, tq=tq, tk=tk, topk=topk),
        grid=(B, nq),
        in_specs=[
            pl.BlockSpec((_A, tq), lambda b, i: (0, b * nq + i)),
            pl.BlockSpec((_A, tq), lambda b, i: (2, b * nq + i)),
            pl.BlockSpec((IDX_HEADS, tq), lambda b, i: (0, b * nq + i)),
            pl.BlockSpec((None, L, _A), lambda b, i: (b, 0, 0), pipeline_mode=once),
            pl.BlockSpec((_A, L), lambda b, i: (1, b), pipeline_mode=once),
            pl.BlockSpec((None, L, LANE), lambda b, i: (b, 0, 2 * _A // LANE), pipeline_mode=once),
            pl.BlockSpec((tk, tk), lambda b, i: (0, 0), pipeline_mode=once),
        ],
        out_specs=pl.BlockSpec((_A, tq), lambda b, i: (0, b * nq + i)),
        out_shape=jax.ShapeDtypeStruct((_A, T), BF16),
        scratch_shapes=[pltpu.VMEM((L // tk, tk, tq), jnp.int32),
                        pltpu.VMEM((L // tk, tk, tq), jnp.int16),
                        pltpu.VMEM((DSA_HEADS, DSA_HEAD_DIM, tq), F32),
                        pltpu.VMEM((DSA_HEADS, SUBLANE, tq), F32)],
        compiler_params=_cparams(("arbitrary", "arbitrary")),
    )(zt, zt, wit, zn3, zt, zn3, sl)


def _split_dot(m, x):
    hi = x.astype(BF16)
    lo = (x - hi.astype(F32)).astype(BF16)
    return _dot(m, hi) + _dot(m, lo)


def _rwkv_body(z_ref, mu_ref, w0_ref, wa_ref, a0_ref, gup_ref, kk_ref, ka_ref, rk_ref,
               lnw_ref, lnb_ref, seg_ref, tri_ref, o_ref, prev_ref, s_ref, *, C):
    c = pl.program_id(1)
    W = RWKV_WIDTH

    @pl.when(c == 0)
    def _():
        prev_ref[...] = jnp.zeros_like(prev_ref)
        s_ref[...] = jnp.zeros_like(s_ref)

    z = z_ref[...]
    R = z.shape[0]
    G = R // C
    rows = lax.broadcasted_iota(jnp.int32, z.shape, 0)
    zp = jnp.where(rows == 0, prev_ref[0:1, :], pltpu.roll(z, 1, 0))
    prev_ref[0:1, :] = z[R - 1:R, :]
    zs = z + (zp - z) * mu_ref[...]
    r = zs[:, 0:W]
    k = zs[:, W:2 * W]
    v = zs[:, 2 * W:3 * W]
    wa = zs[:, 3 * W:3 * W + LANE]
    gc = zs[:, 3 * W + LANE:3 * W + 2 * LANE]

    first_r = lax.broadcasted_iota(jnp.int32, (R, LANE), 1) < RWKV_HEAD_DIM
    first = lax.broadcasted_iota(jnp.int32, (C, LANE), 1) < RWKV_HEAD_DIM
    wa_t = jnp.where(first_r, jnp.tanh(wa), wa)
    lora = _dot(wa_t.astype(BF16), wa_ref[...])
    y = w0_ref[...] + lora[:, :W]
    w_log = -(jnp.maximum(-y, 0.0) + jnp.log(1.0 + jnp.exp(-jnp.abs(y)))) - 0.5
    lw = -jnp.exp(w_log)
    a = jax.nn.sigmoid(a0_ref[...] + lora[:, W:])
    gate = _dot(jax.nn.sigmoid(gc).astype(BF16), gup_ref[...])
    seg = seg_ref[...]
    kk = k * kk_ref[...]
    kk = kk / jnp.maximum(jnp.sqrt(_dot((kk * kk).astype(BF16), seg)), 1e-12)
    k2 = k * (1.0 + (a - 1.0) * ka_ref[...])

    cum = _split_dot(tri_ref[...], lw)
    p_inv = jnp.exp(-cum)
    r_t = (r * jnp.exp(cum)).astype(BF16)
    k_t = (k2 * p_inv).astype(BF16)
    a_t = (-kk * jnp.exp(cum - lw)).astype(BF16)
    b_t = (kk * a * p_inv).astype(BF16)
    vb = v.astype(BF16)

    zero = jnp.zeros((C, LANE), BF16)

    def stack(x):
        return jnp.concatenate([jnp.where(first, x, zero), jnp.where(first, zero, x)], axis=0)

    ri = lax.broadcasted_iota(jnp.int32, (2 * C, 2 * C), 0)
    ci = lax.broadcasted_iota(jnp.int32, (2 * C, 2 * C), 1)
    strict = ci < ri
    incl = ci <= ri
    eye = jnp.where(ci == ri, 1.0, 0.0)
    pairs = RWKV_HEADS // 2

    units = [(g, p) for g in range(G) for p in range(pairs)]
    rw = lambda g: slice(g * C, (g + 1) * C)
    ln = lambda p: slice(p * LANE, (p + 1) * LANE)
    v_st = {u: stack(vb[rw(u[0]), ln(u[1])]) for u in units}
    bk = {u: jnp.concatenate([stack(b_t[rw(u[0]), ln(u[1])]), stack(k_t[rw(u[0]), ln(u[1])])], axis=0)
          for u in units}
    m = {u: _dot_nt(jnp.concatenate([stack(a_t[rw(u[0]), ln(u[1])]), stack(r_t[rw(u[0]), ln(u[1])])],
                                    axis=0), bk[u]) for u in units}
    l_ab = {u: jnp.where(strict, m[u][:2 * C, :2 * C], 0.0) for u in units}
    lv = {u: _dot(jnp.where(strict, m[u][:2 * C, 2 * C:], 0.0).astype(BF16), v_st[u]) for u in units}
    m_rbk = {u: jnp.concatenate([jnp.where(incl, m[u][2 * C:, :2 * C], 0.0),
                                 jnp.where(incl, m[u][2 * C:, 2 * C:], 0.0)], axis=1).astype(BF16)
             for u in units}
    t_inv = {u: eye + l_ab[u] for u in units}
    pw = {u: l_ab[u].astype(BF16) for u in units}
    for _ in range(int(math.log2(C)) - 1):
        pw = {u: _dot(pw[u], pw[u]).astype(BF16) for u in units}
        t_inv = {u: t_inv[u] + _dot(t_inv[u].astype(BF16), pw[u]) for u in units}
    t_inv = {u: t_inv[u].astype(BF16) for u in units}

    state = [s_ref[p] for p in range(pairs)]
    yrows = []
    for g in range(G):
        p_end = jnp.exp(cum[(g + 1) * C - 1:(g + 1) * C, :])
        us = [(g, p) for p in range(pairs)]
        xs = [_dot_nt(jnp.concatenate([a_t[rw(g), ln(p)], r_t[rw(g), ln(p)]], axis=0),
                      state[p].astype(BF16)) for p in range(pairs)]
        x_st = [(stack(xs[p][:C].astype(BF16)).astype(F32) + lv[us[p]]).astype(BF16) for p in range(pairs)]
        uv = [jnp.concatenate([_dot(t_inv[us[p]], x_st[p]).astype(BF16), v_st[us[p]]], axis=0)
              for p in range(pairs)]
        y_st = [_dot(m_rbk[us[p]], uv[p]) for p in range(pairs)]
        state = [(state[p] + _dot_tn(uv[p], bk[us[p]])) * p_end[:, ln(p)] for p in range(pairs)]
        yrows.append(jnp.concatenate([xs[p][C:] + y_st[p][:C] + y_st[p][C:] for p in range(pairs)], axis=1))
    for p in range(pairs):
        s_ref[p] = state[p]
    yy = jnp.concatenate(yrows, axis=0)

    inv_n = 1.0 / RWKV_HEAD_DIM
    mean = _dot(yy.astype(BF16), seg) * inv_n
    d = yy - mean
    var = _dot((d * d).astype(BF16), seg) * inv_n
    yn = d * lax.rsqrt(var + RWKV_GN_EPS) * lnw_ref[...] + lnb_ref[...]
    bonus = _dot((r * k2 * rk_ref[...]).astype(BF16), seg) * v
    o_ref[...] = ((yn + bonus) * gate).astype(o_ref.dtype)


def _rwkv(zr3, mu, w0, w_up, a0, a_up, g_up, k_k, k_a, r_k, ln_w, ln_b):
    B, L, _ = zr3.shape
    C = RWKV_CHUNK
    W = RWKV_WIDTH
    row = lambda t: t.reshape(1, -1)
    wa = jnp.zeros((LANE, 2 * W), F32)
    wa = wa.at[:DECAY_LORA, :W].set(w_up).at[DECAY_LORA:, W:].set(a_up).astype(BF16)
    hid = jnp.arange(W) // RWKV_HEAD_DIM
    seg = (hid[:, None] == hid[None, :]).astype(BF16)
    R = min(RWKV_BLOCK, L)
    tok = jnp.arange(R)
    tri = ((tok[None, :] <= tok[:, None]) & (tok[None, :] // C == tok[:, None] // C)).astype(BF16)
    c2 = lambda b, c: (0, 0)
    vec = lambda n: pl.BlockSpec((1, n), c2)
    return pl.pallas_call(
        functools.partial(_rwkv_body, C=C),
        grid=(B, L // R),
        in_specs=[
            pl.BlockSpec((None, R, _RWKV_W), lambda b, c: (b, c, 0)),
            vec(_RWKV_W), vec(W),
            pl.BlockSpec((LANE, 2 * W), c2),
            vec(W),
            pl.BlockSpec((GATE_LORA, W), c2),
            vec(W), vec(W), vec(W), vec(W), vec(W),
            pl.BlockSpec((W, W), c2),
            pl.BlockSpec((R, R), c2),
        ],
        out_specs=pl.BlockSpec((None, R, W), lambda b, c: (b, c, 0)),
        out_shape=jax.ShapeDtypeStruct((B, L, W), BF16),
        scratch_shapes=[pltpu.VMEM((8, _RWKV_W), F32),
                        pltpu.VMEM((RWKV_HEADS // 2, LANE, LANE), F32)],
        compiler_params=_cparams(("arbitrary", "arbitrary")),
    )(zr3, row(mu), row(w0), wa, row(a0), g_up.astype(BF16), row(k_k), row(k_a), row(r_k),
      row(ln_w), row(ln_b), seg, tri)


def _merge_body(x_ref, yat_ref, yb_ref, yct_ref, g0_ref, g1_ref, g2_ref, wb_ref, wo_ref, gpost_ref, o_ref):
    m = (jax.nn.sigmoid(g0_ref[...].astype(F32)) * _dot_tn(yat_ref[...], wb_ref[0])
         + jax.nn.sigmoid(g1_ref[...].astype(F32)) * _dot(yb_ref[...], wb_ref[1])
         + jax.nn.sigmoid(g2_ref[...].astype(F32)) * _dot_tn(yct_ref[...], wb_ref[2]))
    y = _dot(m.astype(BF16), wo_ref[...])
    o_ref[...] = x_ref[...] + _rms(y, gpost_ref[...])


def _merge(x2, yat, yb, yct, zg, wb, wo, g_post, *, tm):
    T, D = x2.shape
    Wb = yb.shape[1]
    tok = lambda w, j: pl.BlockSpec((tm, w), lambda i: (i, j))
    feat = pl.BlockSpec((Wb, tm), lambda i: (0, i))
    return pl.pallas_call(
        _merge_body,
        grid=(T // tm,),
        in_specs=[
            tok(D, 0), feat, tok(Wb, 0), feat,
            tok(D, 0), tok(D, 1), tok(D, 2),
            pl.BlockSpec((N_BRANCH, Wb, D), lambda i: (0, 0, 0)),
            pl.BlockSpec((D, D), lambda i: (0, 0)),
            pl.BlockSpec((1, D), lambda i: (0, 0)),
        ],
        out_specs=tok(D, 0),
        out_shape=jax.ShapeDtypeStruct((T, D), F32),
        compiler_params=_cparams(("arbitrary",)),
    )(x2, yat, yb, yct, zg, zg, zg, wb, wo, g_post)


def _tiles(B, L):
    T = B * L
    return dict(
        tm_ffn=min(512, T), tf=2816,
        tm_proj=min(1024, T),
        tm_merge=min(512, T),
        tq_dsa=min(256, L), tk_dsa=min(512, L),
        tq_diff=min(512, L), tk_diff=min(512, L),
    )


def _mixer(x2, B, L, l, cfg, mix_norm_pre, mix_norm_post, w_in, rwkv, diffp, w_branch, w_out):
    T, D = x2.shape
    g_pre = mix_norm_pre.reshape(1, D)
    wt, row_scale, wit_w, wn, wr, wg = _split_w_in(w_in)
    tm = cfg["tm_proj"]
    zt, wit = _proj_t(x2, g_pre, wt, row_scale, wit_w, tm=cfg["tm_merge"])
    zn, zr, zg = _proj(x2, g_pre, (wn, wr, wg), (BF16, F32, BF16), tm=cfg["tm_merge"])
    zn3 = zn.reshape(B, L, ZN_W)

    yat = _dsa(zt, wit, zn3, tq=cfg["tq_dsa"], tk=cfg["tk_dsa"])
    yb = _rwkv(zr.reshape(B, L, _RWKV_W), *rwkv)
    lq1, lk1, lq2, lk2, subln = diffp
    lambda_init = 0.8 - 0.6 * math.exp(-0.3 * l)
    rowv = lambda t: t.reshape(1, -1)
    yct = _diff_attn(zt, zn3, rowv(lq1), rowv(lk1), rowv(lq2), rowv(lk2), subln.reshape(-1, 1),
                     lambda_init, tq=cfg["tq_diff"], tk=cfg["tk_diff"])

    return _merge(x2, yat, yb.reshape(T, -1), yct, zg,
                  w_branch.astype(BF16), w_out.astype(BF16), mix_norm_post.reshape(1, D),
                  tm=cfg["tm_merge"])


def kernel(x, ffn1_norm_pre, ffn1_norm_post, ffn1_w_gate, ffn1_w_up, ffn1_w_down,
           mix_norm_pre, mix_norm_post, w_in, rwkv_mu, rwkv_w0, rwkv_w_up, rwkv_a0, rwkv_a_up,
           rwkv_g_up, rwkv_k_k, rwkv_k_a, rwkv_r_k, rwkv_ln_w, rwkv_ln_b,
           diff_lambda_q1, diff_lambda_k1, diff_lambda_q2, diff_lambda_k2, diff_subln,
           w_branch, w_out, ffn2_norm_pre, ffn2_norm_post, ffn2_w_gate, ffn2_w_up, ffn2_w_down):
    B, L, D = x.shape
    depth = w_in.shape[0]
    cfg = _tiles(B, L)
    x2 = x.reshape(B * L, D)
    for l in range(depth):
        x2 = _ffn(x2, ffn1_norm_pre[l].reshape(1, D), ffn1_norm_post[l].reshape(1, D),
                  *_ffn_weights(ffn1_w_gate[l], ffn1_w_up[l], ffn1_w_down[l], cfg["tf"]),
                  tm=cfg["tm_ffn"])
        rwkv = (rwkv_mu[l], rwkv_w0[l], rwkv_w_up[l], rwkv_a0[l], rwkv_a_up[l], rwkv_g_up[l],
                rwkv_k_k[l], rwkv_k_a[l], rwkv_r_k[l], rwkv_ln_w[l], rwkv_ln_b[l])
        diffp = (diff_lambda_q1[l], diff_lambda_k1[l], diff_lambda_q2[l], diff_lambda_k2[l],
                 diff_subln[l])
        x2 = _mixer(x2, B, L, l, cfg, mix_norm_pre[l], mix_norm_post[l], w_in[l], rwkv, diffp,
                    w_branch[l], w_out[l])
        x2 = _ffn(x2, ffn2_norm_pre[l].reshape(1, D), ffn2_norm_post[l].reshape(1, D),
                  *_ffn_weights(ffn2_w_gate[l], ffn2_w_up[l], ffn2_w_down[l], cfg["tf"]),
                  tm=cfg["tm_ffn"])
    return x2.reshape(B, L, D)
```

```python
import functools
import math

import jax
import jax.numpy as jnp
from jax import lax
from jax.experimental import pallas as pl
from jax.experimental.pallas import tpu as pltpu

F32 = jnp.float32
BF16 = jnp.bfloat16

D_MODEL = 1024
D_FF = 2816
DSA_HEADS = 4
DSA_HEAD_DIM = 128
IDX_HEADS = 8
IDX_DIM = 64
TOPK_MAX = 256
RWKV_HEADS = 8
RWKV_HEAD_DIM = 64
RWKV_WIDTH = RWKV_HEADS * RWKV_HEAD_DIM
DECAY_LORA = 64
AAA_LORA = 64
GATE_LORA = 128
RWKV_GN_EPS = 64e-5
DIFF_HEADS = 4
DIFF_QK_DIM = 64
DIFF_V_DIM = 128
N_BRANCH = 3
BRANCH_WIDTH = 512
NORM_EPS = 1e-6

_A = DSA_HEADS * DSA_HEAD_DIM
_DSA_W = 3 * _A + IDX_HEADS * IDX_DIM + IDX_DIM + IDX_HEADS
_RWKV_W = 3 * RWKV_WIDTH + DECAY_LORA + AAA_LORA + GATE_LORA
_DIFF_W = 3 * 512
_GATE_W = N_BRANCH * D_MODEL

LANE = 128
SUBLANE = 8
VMEM_LIMIT = 56 * 1024 * 1024
NEG_BIG = -1e30
INT_MIN = -2 ** 31

ZT_ROWS = 5 * _A
ZN_W = 2 * _A + LANE

RWKV_CHUNK = 64
RWKV_BLOCK = 256
ATT_SUB = 128
DIFF_SUB = 256


def _cparams(sem):
    return pltpu.CompilerParams(dimension_semantics=sem, vmem_limit_bytes=VMEM_LIMIT)


def _rms(x, g):
    return x * lax.rsqrt(jnp.mean(x * x, axis=-1, keepdims=True) + NORM_EPS) * g


def _dot(a, b):
    return jnp.dot(a, b, preferred_element_type=F32)


def _dot_nt(a, b):
    return lax.dot_general(a, b, (((1,), (1,)), ((), ())), preferred_element_type=F32)


def _dot_tn(a, b):
    return lax.dot_general(a, b, (((0,), (0,)), ((), ())), preferred_element_type=F32)


def _ffn_body(x_ref, gpre_ref, gpost_ref, wg_ref, wu_ref, wd_ref, o_ref, h_ref, acc_ref, *, nf):
    x = x_ref[...]
    h_ref[...] = _rms(x, gpre_ref[...]).astype(BF16)
    acc_ref[...] = jnp.zeros_like(acc_ref)

    def chunk(c, carry):
        h = h_ref[...]
        g = _dot(h, wg_ref[c])
        u = _dot(h, wu_ref[c])
        a = (g * jax.nn.sigmoid(g) * u).astype(BF16)
        acc_ref[...] += _dot(a, wd_ref[c])
        return carry

    lax.fori_loop(0, nf, chunk, 0)
    o_ref[...] = x + 0.5 * _rms(acc_ref[...], gpost_ref[...])


def _ffn(x2, g_pre, g_post, wg, wu, wd, *, tm):
    T, D = x2.shape
    nf, _, tf = wg.shape
    const3 = lambda i: (0, 0, 0)
    return pl.pallas_call(
        functools.partial(_ffn_body, nf=nf),
        grid=(T // tm,),
        in_specs=[
            pl.BlockSpec((tm, D), lambda i: (i, 0)),
            pl.BlockSpec((1, D), lambda i: (0, 0)),
            pl.BlockSpec((1, D), lambda i: (0, 0)),
            pl.BlockSpec((nf, D, tf), const3, pipeline_mode=pl.Buffered(1)),
            pl.BlockSpec((nf, D, tf), const3, pipeline_mode=pl.Buffered(1)),
            pl.BlockSpec((nf, tf, D), const3, pipeline_mode=pl.Buffered(1)),
        ],
        out_specs=pl.BlockSpec((tm, D), lambda i: (i, 0)),
        out_shape=jax.ShapeDtypeStruct((T, D), F32),
        scratch_shapes=[pltpu.VMEM((tm, D), BF16), pltpu.VMEM((tm, D), F32)],
        compiler_params=_cparams(("arbitrary",)),
    )(x2, g_pre, g_post, wg, wu, wd)


def _ffn_weights(w_gate, w_up, w_down, tf):
    D, F = w_gate.shape
    nf = F // tf
    wg = w_gate.astype(BF16).reshape(D, nf, tf).transpose(1, 0, 2)
    wu = w_up.astype(BF16).reshape(D, nf, tf).transpose(1, 0, 2)
    wd = w_down.astype(BF16).reshape(nf, tf, D)
    return wg, wu, wd


def _proj_body(x_ref, g_ref, *refs):
    n = len(refs) // 2
    h = _rms(x_ref[...], g_ref[...]).astype(BF16)
    for w_ref, o_ref in zip(refs[:n], refs[n:]):
        o_ref[...] = _dot(h, w_ref[...]).astype(o_ref.dtype)


def _proj(x2, g, ws, out_dtypes, *, tm):
    T, D = x2.shape
    return pl.pallas_call(
        _proj_body,
        grid=(T // tm,),
        in_specs=[pl.BlockSpec((tm, D), lambda i: (i, 0)), pl.BlockSpec((1, D), lambda i: (0, 0))]
        + [pl.BlockSpec(w.shape, lambda i: (0, 0), pipeline_mode=pl.Buffered(1)) for w in ws],
        out_specs=[pl.BlockSpec((tm, w.shape[1]), lambda i: (i, 0)) for w in ws],
        out_shape=[jax.ShapeDtypeStruct((T, w.shape[1]), dt) for w, dt in zip(ws, out_dtypes)],
        compiler_params=_cparams(("arbitrary",)),
    )(x2, g, *ws)


def _proj_t_body(x_ref, g_ref, wt_ref, sc_ref, wit_ref, o_ref, wi_ref):
    h = _rms(x_ref[...], g_ref[...]).astype(BF16)
    wi_ref[...] = _dot_nt(wit_ref[...], h)
    o_ref[...] = (_dot_nt(wt_ref[...], h) * sc_ref[...]).astype(o_ref.dtype)


def _proj_t(x2, g, wt, row_scale, wit, *, tm):
    T, D = x2.shape
    N = wt.shape[0]
    once = pl.Buffered(1)
    return pl.pallas_call(
        _proj_t_body,
        grid=(T // tm,),
        in_specs=[
            pl.BlockSpec((tm, D), lambda i: (i, 0)),
            pl.BlockSpec((1, D), lambda i: (0, 0)),
            pl.BlockSpec((N, D), lambda i: (0, 0), pipeline_mode=once),
            pl.BlockSpec((N, 1), lambda i: (0, 0), pipeline_mode=once),
            pl.BlockSpec((IDX_HEADS, D), lambda i: (0, 0), pipeline_mode=once),
        ],
        out_specs=[pl.BlockSpec((N, tm), lambda i: (0, i)),
                   pl.BlockSpec((IDX_HEADS, tm), lambda i: (0, i))],
        out_shape=[jax.ShapeDtypeStruct((N, T), BF16), jax.ShapeDtypeStruct((IDX_HEADS, T), F32)],
        compiler_params=_cparams(("arbitrary",)),
    )(x2, g, wt, row_scale, wit)


def _split_w_in(w_in):
    D = w_in.shape[0]
    c = 0
    dsa = w_in[:, c:c + _DSA_W]; c += _DSA_W
    rw = w_in[:, c:c + _RWKV_W]; c += _RWKV_W
    df = w_in[:, c:c + _DIFF_W]; c += _DIFF_W
    gate = w_in[:, c:c + _GATE_W]
    q, k, v, qi = (dsa[:, j * _A:(j + 1) * _A] for j in range(4))
    ki = dsa[:, 4 * _A:4 * _A + IDX_DIM]
    wi = dsa[:, 4 * _A + IDX_DIM:]
    dq, dk, dv = (df[:, j * 512:(j + 1) * 512] for j in range(3))
    wt = jnp.concatenate([q, v, qi, dq, dv], axis=1).T.astype(BF16)
    ones = jnp.ones((_A,), F32)
    log2e = math.log2(math.e)
    row_scale = jnp.concatenate([ones * (DSA_HEAD_DIM ** -0.5 * log2e), ones, ones,
                                 ones * (DIFF_QK_DIM ** -0.5 * log2e), ones]).reshape(ZT_ROWS, 1)
    wn = jnp.concatenate([k, dk, ki, jnp.zeros((D, LANE - IDX_DIM), F32)], axis=1).astype(BF16)
    return wt, row_scale, wi.T.astype(BF16), wn, rw.astype(BF16), gate.astype(BF16)


def _fori_by_two(n, body, init):
    carry = lax.fori_loop(0, n // 2, lambda j, cr: body(2 * j + 1, body(2 * j, cr)), init)
    return lax.fori_loop(2 * (n // 2), n, body, carry)


def _fold_rows(x, op, group=SUBLANE):
    r, n = x.shape
    parts = x.reshape(r // group, group, n)
    while parts.shape[0] > 1:
        pairs = parts.reshape(parts.shape[0] // 2, 2, group, n)
        parts = op(pairs[:, 0], pairs[:, 1])
    return parts[0]


def _col_max(x):
    return jnp.max(_fold_rows(x, jnp.maximum), axis=0, keepdims=True)


def _col_sum(x):
    return jnp.sum(_fold_rows(x, jnp.add), axis=0, keepdims=True)


def _softmax_streams(scores, values_t, n, carry, acc_ref):
    out = []
    nxt = [piece() for piece in scores(0)]
    for j in range(n):
        ss = nxt
        later = scores(j + 1) if j + 1 < n else []
        m_old, l_old = carry[2 * j], carry[2 * j + 1]
        mx = functools.reduce(jnp.maximum, [_fold_rows(s, jnp.maximum) for s in ss])
        m_new = jnp.maximum(m_old, jnp.max(mx, axis=0, keepdims=True))
        alpha = jnp.exp2(m_old - m_new)
        ps, nxt = [], []
        for r, s in enumerate(ss):
            if r < len(later):
                nxt.append(later[r]())
            ps.append(jnp.exp2(s - m_new))
        sm = functools.reduce(jnp.add, [_fold_rows(p, jnp.add) for p in ps])
        l_new = alpha * l_old + jnp.sum(sm, axis=0, keepdims=True)
        vt = values_t(j)
        sub = ps[0].shape[0]
        pv = functools.reduce(jnp.add, [
            _dot(vt[:, r * sub:(r + 1) * sub], p.astype(BF16)) for r, p in enumerate(ps)])
        acc_ref[j] = alpha * acc_ref[j] + pv
        out += [m_new, l_new]
    return tuple(out)


def _diff_body(qt_ref, k_ref, vt_ref, lq1_ref, lk1_ref, lq2_ref, lk2_ref, sub_ref, o_ref,
               acc_ref, *, tq, tk, lambda_init):
    i = pl.program_id(1)
    qt = qt_ref[...]
    first = (lax.broadcasted_iota(jnp.int32, qt.shape, 0) & DIFF_QK_DIM) == 0
    zero = jnp.zeros_like(qt)
    qts = (jnp.where(first, qt, zero), jnp.where(first, zero, qt))
    acc_ref[...] = jnp.zeros_like(acc_ref)
    n_full = (i * tq) // tk

    def step(c, carry, masked):
        off = pl.multiple_of(c * tk, tk)
        k = k_ref[pl.ds(off, tk), :]
        if masked:
            key = c * tk + lax.broadcasted_iota(jnp.int32, (tk, tq), 0)
            qry = i * tq + lax.broadcasted_iota(jnp.int32, (tk, tq), 1)
            keep = key <= qry
        hsl = lambda j: slice((j // 2) * LANE, (j // 2 + 1) * LANE)

        def piece(j, r):
            rs = slice(r * DIFF_SUB, (r + 1) * DIFF_SUB)
            s = _dot(k[rs, hsl(j)], qts[j % 2][hsl(j), :])
            return jnp.where(keep[rs], s, NEG_BIG) if masked else s

        def scores(j):
            return [functools.partial(piece, j, r) for r in range(tk // DIFF_SUB)]

        return _softmax_streams(scores, lambda j: vt_ref[hsl(j), pl.ds(off, tk)], 2 * DIFF_HEADS,
                                carry, acc_ref)

    init = (jnp.full((1, tq), NEG_BIG, F32), jnp.zeros((1, tq), F32)) * (2 * DIFF_HEADS)
    carry = _fori_by_two(n_full, lambda c, cr: step(c, cr, False), init)
    fin = step(n_full, carry, True)

    lam = (jnp.exp(jnp.sum(lq1_ref[...] * lk1_ref[...], axis=1, keepdims=True))
           - jnp.exp(jnp.sum(lq2_ref[...] * lk2_ref[...], axis=1, keepdims=True)) + lambda_init)
    gain = sub_ref[...] * (1.0 - lambda_init)
    for h in range(DIFF_HEADS):
        o = acc_ref[2 * h] / fin[4 * h + 1] - lam * (acc_ref[2 * h + 1] / fin[4 * h + 3])
        ms = jnp.sum(o * o, axis=0, keepdims=True) * (1.0 / DIFF_V_DIM)
        o_ref[h * LANE:(h + 1) * LANE, :] = (o * lax.rsqrt(ms + NORM_EPS) * gain).astype(o_ref.dtype)


def _diff_attn(zt, zn3, lq1, lk1, lq2, lk2, subln, lambda_init, *, tq, tk):
    B, L, _ = zn3.shape
    T = B * L
    nq = L // tq
    vec = lambda n: pl.BlockSpec((1, n), lambda b, i: (0, 0))
    once = pl.Buffered(1)
    return pl.pallas_call(
        functools.partial(_diff_body, tq=tq, tk=tk, lambda_init=lambda_init),
        grid=(B, nq),
        in_specs=[
            pl.BlockSpec((_A, tq), lambda b, i: (3, b * nq + i)),
            pl.BlockSpec((None, L, _A), lambda b, i: (b, 0, 1), pipeline_mode=once),
            pl.BlockSpec((_A, L), lambda b, i: (4, b), pipeline_mode=once),
            vec(DIFF_QK_DIM), vec(DIFF_QK_DIM), vec(DIFF_QK_DIM), vec(DIFF_QK_DIM),
            pl.BlockSpec((DIFF_V_DIM, 1), lambda b, i: (0, 0)),
        ],
        out_specs=pl.BlockSpec((_A, tq), lambda b, i: (0, b * nq + i)),
        out_shape=jax.ShapeDtypeStruct((DIFF_HEADS * DIFF_V_DIM, T), BF16),
        scratch_shapes=[pltpu.VMEM((2 * DIFF_HEADS, DIFF_V_DIM, tq), F32)],
        compiler_params=_cparams(("arbitrary", "arbitrary")),
    )(zt, zn3, zt, lq1, lk1, lq2, lk2, subln)


def _sortable(x):
    b = pltpu.bitcast(x, jnp.int32)
    return b ^ ((b >> 31) & jnp.int32(0x7FFFFFFF))


def _dsa_body(qt_ref, qit_ref, wit_ref, k_ref, vt_ref, ki_ref, sl_ref, o_ref,
              keys_ref, top_ref, acc_ref, l_ref, *, tq, tk, topk):
    i = pl.program_id(1)
    nkt = ((i + 1) * tq + tk - 1) // tk
    key0 = lax.broadcasted_iota(jnp.int32, (tk, tq), 0)
    qry = i * tq + lax.broadcasted_iota(jnp.int32, (tk, tq), 1)

    qit = qit_ref[...]
    w = wit_ref[...] * (IDX_HEADS ** -0.5 * IDX_DIM ** -0.5)

    def score_tile(c, carry):
        ki = ki_ref[pl.ds(pl.multiple_of(c * tk, tk), tk), :][:, :IDX_DIM]
        scs = []
        for r in range(tk // ATT_SUB):
            sc = jnp.zeros((ATT_SUB, tq), F32)
            for h in range(IDX_HEADS):
                d = _dot(ki[r * ATT_SUB:(r + 1) * ATT_SUB], qit[h * IDX_DIM:(h + 1) * IDX_DIM, :])
                sc = sc + jnp.maximum(d, 0.0) * w[h:h + 1, :]
            scs.append(sc)
        sc = jnp.concatenate(scs, axis=0)
        key = jnp.where(c * tk + key0 <= qry, _sortable(sc), INT_MIN)
        keys_ref[c] = key
        top_ref[c] = (key >> 16).astype(jnp.int16)
        return carry

    _fori_by_two(nkt, score_tile, 0)

    def count_where(pred):
        def tile(c, acc):
            return acc + _fold_rows(jnp.where(pred(keys_ref[c]), 1.0, 0.0), jnp.add)
        acc = lax.fori_loop(0, nkt, tile, jnp.zeros((SUBLANE, tq), F32))
        return jnp.sum(acc, axis=0, keepdims=True)

    one16, zero16 = jnp.ones((), jnp.int16), jnp.zeros((), jnp.int16)

    def count_half(half):
        def tile(c, acc):
            hit = jnp.where(top_ref[c] >= half, one16, zero16)
            return acc + _fold_rows(hit, jnp.add, 2 * SUBLANE).astype(F32)
        acc = _fori_by_two(nkt, tile, jnp.zeros((2 * SUBLANE, tq), F32))
        return jnp.sum(acc, axis=0, keepdims=True)

    def upper_pass(p, st):
        t, done = st
        cand = t ^ lax.shift_left(jnp.int32(1), 31 - p)
        cnt = count_half((cand >> 16).astype(jnp.int16))
        take = jnp.logical_and(cnt >= topk, done < 0.5)
        return jnp.where(take, cand, t), jnp.where(jnp.logical_and(take, cnt == topk), 1.0, done)

    qpos = i * tq + lax.broadcasted_iota(jnp.int32, (1, tq), 1)
    thr, done = lax.fori_loop(0, 16, upper_pass, (jnp.full((1, tq), INT_MIN, jnp.int32),
                                                  jnp.where(qpos < topk, 1.0, 0.0)))

    upper = (thr >> 16).astype(jnp.int16)
    low_min = jnp.int16(-32768)

    def lower_tile(c, acc):
        top = top_ref[c]
        low = ((keys_ref[c] & jnp.int32(0xFFFF)) - 32768).astype(jnp.int16)
        top_ref[c] = jnp.where(top == upper, low, low_min)
        return acc + _fold_rows(jnp.where(top > upper, one16, zero16), jnp.add, 2 * SUBLANE).astype(F32)
    above = jnp.sum(_fori_by_two(nkt, lower_tile, jnp.zeros((2 * SUBLANE, tq), F32)),
                    axis=0, keepdims=True)

    def pending(done):
        return (jnp.min(done) < 0.5).astype(jnp.int32)

    def lower_cond(st):
        return jnp.logical_and(st[0] < 32, st[3] > 0)

    def lower_pass(st):
        p, t, done, _ = st
        for b in range(2):
            cand = t ^ lax.shift_left(jnp.int32(1), 31 - p - b)
            cnt = above + count_half(((cand & jnp.int32(0xFFFF)) - 32768).astype(jnp.int16))
            take = jnp.logical_and(cnt >= topk, done < 0.5)
            t = jnp.where(take, cand, t)
            done = jnp.where(jnp.logical_and(take, cnt == topk), 1.0, done)
        return p + 2, t, done, pending(done)

    _, thr, _, ties = lax.while_loop(lower_cond, lower_pass, (jnp.int32(16), thr, done, pending(done)))
    thr = jnp.maximum(thr, INT_MIN + 1)

    qt = qt_ref[...]
    acc_ref[...] = jnp.zeros_like(acc_ref)

    def attend(c, bias, carry):
        off = pl.multiple_of(c * tk, tk)
        k = k_ref[pl.ds(off, tk), :]
        hsl = lambda h: slice(h * DSA_HEAD_DIM, (h + 1) * DSA_HEAD_DIM)

        def piece(h, r):
            rs = slice(r * ATT_SUB, (r + 1) * ATT_SUB)
            return _dot(k[rs, hsl(h)], qt[hsl(h), :]) + bias[rs]

        def scores(h):
            return [functools.partial(piece, h, r) for r in range(tk // ATT_SUB)]

        return _softmax_streams(scores, lambda h: vt_ref[hsl(h), pl.ds(off, tk)], DSA_HEADS, carry, acc_ref)

    init = (jnp.full((1, tq), NEG_BIG, F32), jnp.zeros((1, tq), F32)) * DSA_HEADS

    def finish(fin):
        for h in range(DSA_HEADS):
            l_ref[h] = jnp.broadcast_to(fin[2 * h + 1], (SUBLANE, tq))

    @pl.when(ties == 0)
    def _():
        def tile(c, carry):
            return attend(c, jnp.where(keys_ref[c] >= thr, 0.0, NEG_BIG), carry)
        finish(_fori_by_two(nkt, tile, init))

    @pl.when(ties != 0)
    def _():
        need = topk - count_where(lambda k: k > thr)

        def tile(c, carry):
            key = keys_ref[c]
            eq = key == thr
            eqf = jnp.where(eq, 1.0, 0.0)
            rank = carry[0] + _dot(sl_ref[...], eqf.astype(BF16))
            tie_ok = jnp.where(eq, jnp.where(rank < need, 0.0, NEG_BIG), NEG_BIG)
            bias = jnp.where(key > thr, 0.0, tie_ok)
            return (carry[0] + _col_sum(eqf),) + attend(c, bias, carry[1:])

        finish(lax.fori_loop(0, nkt, tile, (jnp.zeros((1, tq), F32),) + init)[1:])

    for h in range(DSA_HEADS):
        hs = slice(h * DSA_HEAD_DIM, (h + 1) * DSA_HEAD_DIM)
        o_ref[hs, :] = (acc_ref[h] / l_ref[h][0:1, :]).astype(o_ref.dtype)


def _dsa(zt, wit, zn3, *, tq, tk):
    B, L, _ = zn3.shape
    T = B * L
    nq = L // tq
    topk = min(TOPK_MAX, L // 4)
    sl = (lax.broadcasted_iota(jnp.int32, (tk, tk), 1)
          < lax.broadcasted_iota(jnp.int32, (tk, tk), 0)).astype(BF16)
    once = pl.Buffered(1)
    return pl.pallas_call(
        functools.partial(_dsa_body, tq=tq, tk=tk, topk=topk),
        grid=(B, nq),
        in_specs=[
            pl.BlockSpec((_A, tq), lambda b, i: (0, b * nq + i)),
            pl.BlockSpec((_A, tq), lambda b, i: (2, b * nq + i)),
            pl.BlockSpec((IDX_HEADS, tq), lambda b, i: (0, b * nq + i)),
            pl.BlockSpec((None, L, _A), lambda b, i: (b, 0, 0), pipeline_mode=once),
            pl.BlockSpec((_A, L), lambda b, i: (1, b), pipeline_mode=once),
            pl.BlockSpec((None, L, LANE), lambda b, i: (b, 0, 2 * _A // LANE), pipeline_mode=once),
            pl.BlockSpec((tk, tk), lambda b, i: (0, 0), pipeline_mode=once),
        ],
        out_specs=pl.BlockSpec((_A, tq), lambda b, i: (0, b * nq + i)),
        out_shape=jax.ShapeDtypeStruct((_A, T), BF16),
        scratch_shapes=[pltpu.VMEM((L // tk, tk, tq), jnp.int32),
                        pltpu.VMEM((L // tk, tk, tq), jnp.int16),
                        pltpu.VMEM((DSA_HEADS, DSA_HEAD_DIM, tq), F32),
                        pltpu.VMEM((DSA_HEADS, SUBLANE, tq), F32)],
        compiler_params=_cparams(("arbitrary", "arbitrary")),
    )(zt, zt, wit, zn3, zt, zn3, sl)


def _split_dot(m, x):
    hi = x.astype(BF16)
    lo = (x - hi.astype(F32)).astype(BF16)
    return _dot(m, hi) + _dot(m, lo)


def _rwkv_body(z_ref, mu_ref, w0_ref, wa_ref, a0_ref, gup_ref, kk_ref, ka_ref, rk_ref,
               lnw_ref, lnb_ref, seg_ref, tri_ref, o_ref, prev_ref, s_ref, *, C):
    c = pl.program_id(1)
    W = RWKV_WIDTH

    @pl.when(c == 0)
    def _():
        prev_ref[...] = jnp.zeros_like(prev_ref)
        s_ref[...] = jnp.zeros_like(s_ref)

    z = z_ref[...]
    R = z.shape[0]
    G = R // C
    rows = lax.broadcasted_iota(jnp.int32, z.shape, 0)
    zp = jnp.where(rows == 0, prev_ref[0:1, :], pltpu.roll(z, 1, 0))
    prev_ref[0:1, :] = z[R - 1:R, :]
    zs = z + (zp - z) * mu_ref[...]
    r = zs[:, 0:W]
    k = zs[:, W:2 * W]
    v = zs[:, 2 * W:3 * W]
    wa = zs[:, 3 * W:3 * W + LANE]
    gc = zs[:, 3 * W + LANE:3 * W + 2 * LANE]

    first_r = lax.broadcasted_iota(jnp.int32, (R, LANE), 1) < RWKV_HEAD_DIM
    first = lax.broadcasted_iota(jnp.int32, (C, LANE), 1) < RWKV_HEAD_DIM
    wa_t = jnp.where(first_r, jnp.tanh(wa), wa)
    lora = _dot(wa_t.astype(BF16), wa_ref[...])
    y = w0_ref[...] + lora[:, :W]
    w_log = -(jnp.maximum(-y, 0.0) + jnp.log(1.0 + jnp.exp(-jnp.abs(y)))) - 0.5
    lw = -jnp.exp(w_log)
    a = jax.nn.sigmoid(a0_ref[...] + lora[:, W:])
    gate = _dot(jax.nn.sigmoid(gc).astype(BF16), gup_ref[...])
    seg = seg_ref[...]
    kk = k * kk_ref[...]
    kk = kk / jnp.maximum(jnp.sqrt(_dot((kk * kk).astype(BF16), seg)), 1e-12)
    k2 = k * (1.0 + (a - 1.0) * ka_ref[...])

    cum = _split_dot(tri_ref[...], lw)
    p_inv = jnp.exp(-cum)
    r_t = (r * jnp.exp(cum)).astype(BF16)
    k_t = (k2 * p_inv).astype(BF16)
    a_t = (-kk * jnp.exp(cum - lw)).astype(BF16)
    b_t = (kk * a * p_inv).astype(BF16)
    vb = v.astype(BF16)

    zero = jnp.zeros((C, LANE), BF16)

    def stack(x):
        return jnp.concatenate([jnp.where(first, x, zero), jnp.where(first, zero, x)], axis=0)

    ri = lax.broadcasted_iota(jnp.int32, (2 * C, 2 * C), 0)
    ci = lax.broadcasted_iota(jnp.int32, (2 * C, 2 * C), 1)
    strict = ci < ri
    incl = ci <= ri
    eye = jnp.where(ci == ri, 1.0, 0.0)
    pairs = RWKV_HEADS // 2

    units = [(g, p) for g in range(G) for p in range(pairs)]
    rw = lambda g: slice(g * C, (g + 1) * C)
    ln = lambda p: slice(p * LANE, (p + 1) * LANE)
    v_st = {u: stack(vb[rw(u[0]), ln(u[1])]) for u in units}
    bk = {u: jnp.concatenate([stack(b_t[rw(u[0]), ln(u[1])]), stack(k_t[rw(u[0]), ln(u[1])])], axis=0)
          for u in units}
    m = {u: _dot_nt(jnp.concatenate([stack(a_t[rw(u[0]), ln(u[1])]), stack(r_t[rw(u[0]), ln(u[1])])],
                                    axis=0), bk[u]) for u in units}
    l_ab = {u: jnp.where(strict, m[u][:2 * C, :2 * C], 0.0) for u in units}
    lv = {u: _dot(jnp.where(strict, m[u][:2 * C, 2 * C:], 0.0).astype(BF16), v_st[u]) for u in units}
    m_rbk = {u: jnp.concatenate([jnp.where(incl, m[u][2 * C:, :2 * C], 0.0),
                                 jnp.where(incl, m[u][2 * C:, 2 * C:], 0.0)], axis=1).astype(BF16)
             for u in units}
    t_inv = {u: eye + l_ab[u] for u in units}
    pw = {u: l_ab[u].astype(BF16) for u in units}
    for _ in range(int(math.log2(C)) - 1):
        pw = {u: _dot(pw[u], pw[u]).astype(BF16) for u in units}
        t_inv = {u: t_inv[u] + _dot(t_inv[u].astype(BF16), pw[u]) for u in units}
    t_inv = {u: t_inv[u].astype(BF16) for u in units}

    state = [s_ref[p] for p in range(pairs)]
    yrows = []
    for g in range(G):
        p_end = jnp.exp(cum[(g + 1) * C - 1:(g + 1) * C, :])
        us = [(g, p) for p in range(pairs)]
        xs = [_dot_nt(jnp.concatenate([a_t[rw(g), ln(p)], r_t[rw(g), ln(p)]], axis=0),
                      state[p].astype(BF16)) for p in range(pairs)]
        x_st = [(stack(xs[p][:C].astype(BF16)).astype(F32) + lv[us[p]]).astype(BF16) for p in range(pairs)]
        uv = [jnp.concatenate([_dot(t_inv[us[p]], x_st[p]).astype(BF16), v_st[us[p]]], axis=0)
              for p in range(pairs)]
        y_st = [_dot(m_rbk[us[p]], uv[p]) for p in range(pairs)]
        state = [(state[p] + _dot_tn(uv[p], bk[us[p]])) * p_end[:, ln(p)] for p in range(pairs)]
        yrows.append(jnp.concatenate([xs[p][C:] + y_st[p][:C] + y_st[p][C:] for p in range(pairs)], axis=1))
    for p in range(pairs):
        s_ref[p] = state[p]
    yy = jnp.concatenate(yrows, axis=0)

    inv_n = 1.0 / RWKV_HEAD_DIM
    mean = _dot(yy.astype(BF16), seg) * inv_n
    d = yy - mean
    var = _dot((d * d).astype(BF16), seg) * inv_n
    yn = d * lax.rsqrt(var + RWKV_GN_EPS) * lnw_ref[...] + lnb_ref[...]
    bonus = _dot((r * k2 * rk_ref[...]).astype(BF16), seg) * v
    o_ref[...] = ((yn + bonus) * gate).astype(o_ref.dtype)


def _rwkv(zr3, mu, w0, w_up, a0, a_up, g_up, k_k, k_a, r_k, ln_w, ln_b):
    B, L, _ = zr3.shape
    C = RWKV_CHUNK
    W = RWKV_WIDTH
    row = lambda t: t.reshape(1, -1)
    wa = jnp.zeros((LANE, 2 * W), F32)
    wa = wa.at[:DECAY_LORA, :W].set(w_up).at[DECAY_LORA:, W:].set(a_up).astype(BF16)
    hid = jnp.arange(W) // RWKV_HEAD_DIM
    seg = (hid[:, None] == hid[None, :]).astype(BF16)
    R = min(RWKV_BLOCK, L)
    tok = jnp.arange(R)
    tri = ((tok[None, :] <= tok[:, None]) & (tok[None, :] // C == tok[:, None] // C)).astype(BF16)
    c2 = lambda b, c: (0, 0)
    vec = lambda n: pl.BlockSpec((1, n), c2)
    return pl.pallas_call(
        functools.partial(_rwkv_body, C=C),
        grid=(B, L // R),
        in_specs=[
            pl.BlockSpec((None, R, _RWKV_W), lambda b, c: (b, c, 0)),
            vec(_RWKV_W), vec(W),
            pl.BlockSpec((LANE, 2 * W), c2),
            vec(W),
            pl.BlockSpec((GATE_LORA, W), c2),
            vec(W), vec(W), vec(W), vec(W), vec(W),
            pl.BlockSpec((W, W), c2),
            pl.BlockSpec((R, R), c2),
        ],
        out_specs=pl.BlockSpec((None, R, W), lambda b, c: (b, c, 0)),
        out_shape=jax.ShapeDtypeStruct((B, L, W), BF16),
        scratch_shapes=[pltpu.VMEM((8, _RWKV_W), F32),
                        pltpu.VMEM((RWKV_HEADS // 2, LANE, LANE), F32)],
        compiler_params=_cparams(("arbitrary", "arbitrary")),
    )(zr3, row(mu), row(w0), wa, row(a0), g_up.astype(BF16), row(k_k), row(k_a), row(r_k),
      row(ln_w), row(ln_b), seg, tri)


def _merge_body(x_ref, yat_ref, yb_ref, yct_ref, g0_ref, g1_ref, g2_ref, wb_ref, wo_ref, gpost_ref, o_ref):
    m = (jax.nn.sigmoid(g0_ref[...].astype(F32)) * _dot_tn(yat_ref[...], wb_ref[0])
         + jax.nn.sigmoid(g1_ref[...].astype(F32)) * _dot(yb_ref[...], wb_ref[1])
         + jax.nn.sigmoid(g2_ref[...].astype(F32)) * _dot_tn(yct_ref[...], wb_ref[2]))
    y = _dot(m.astype(BF16), wo_ref[...])
    o_ref[...] = x_ref[...] + _rms(y, gpost_ref[...])


def _merge(x2, yat, yb, yct, zg, wb, wo, g_post, *, tm):
    T, D = x2.shape
    Wb = yb.shape[1]
    tok = lambda w, j: pl.BlockSpec((tm, w), lambda i: (i, j))
    feat = pl.BlockSpec((Wb, tm), lambda i: (0, i))
    return pl.pallas_call(
        _merge_body,
        grid=(T // tm,),
        in_specs=[
            tok(D, 0), feat, tok(Wb, 0), feat,
            tok(D, 0), tok(D, 1), tok(D, 2),
            pl.BlockSpec((N_BRANCH, Wb, D), lambda i: (0, 0, 0)),
            pl.BlockSpec((D, D), lambda i: (0, 0)),
            pl.BlockSpec((1, D), lambda i: (0, 0)),
        ],
        out_specs=tok(D, 0),
        out_shape=jax.ShapeDtypeStruct((T, D), F32),
        compiler_params=_cparams(("arbitrary",)),
    )(x2, yat, yb, yct, zg, zg, zg, wb, wo, g_post)


def _tiles(B, L):
    T = B * L
    return dict(
        tm_ffn=min(512, T), tf=2816,
        tm_proj=min(1024, T),
        tm_merge=min(512, T),
        tq_dsa=min(256, L), tk_dsa=min(512, L),
        tq_diff=min(512, L), tk_diff=min(512, L),
    )


def _mixer(x2, B, L, l, cfg, mix_norm_pre, mix_norm_post, w_in, rwkv, diffp, w_branch, w_out):
    T, D = x2.shape
    g_pre = mix_norm_pre.reshape(1, D)
    wt, row_scale, wit_w, wn, wr, wg = _split_w_in(w_in)
    tm = cfg["tm_proj"]
    zt, wit = _proj_t(x2, g_pre, wt, row_scale, wit_w, tm=cfg["tm_merge"])
    zn, zr, zg = _proj(x2, g_pre, (wn, wr, wg), (BF16, F32, BF16), tm=cfg["tm_merge"])
    zn3 = zn.reshape(B, L, ZN_W)

    yat = _dsa(zt, wit, zn3, tq=cfg["tq_dsa"], tk=cfg["tk_dsa"])
    yb = _rwkv(zr.reshape(B, L, _RWKV_W), *rwkv)
    lq1, lk1, lq2, lk2, subln = diffp
    lambda_init = 0.8 - 0.6 * math.exp(-0.3 * l)
    rowv = lambda t: t.reshape(1, -1)
    yct = _diff_attn(zt, zn3, rowv(lq1), rowv(lk1), rowv(lq2), rowv(lk2), subln.reshape(-1, 1),
                     lambda_init, tq=cfg["tq_diff"], tk=cfg["tk_diff"])

    return _merge(x2, yat, yb.reshape(T, -1), yct, zg,
                  w_branch.astype(BF16), w_out.astype(BF16), mix_norm_post.reshape(1, D),
                  tm=cfg["tm_merge"])


def kernel(x, ffn1_norm_pre, ffn1_norm_post, ffn1_w_gate, ffn1_w_up, ffn1_w_down,
           mix_norm_pre, mix_norm_post, w_in, rwkv_mu, rwkv_w0, rwkv_w_up, rwkv_a0, rwkv_a_up,
           rwkv_g_up, rwkv_k_k, rwkv_k_a, rwkv_r_k, rwkv_ln_w, rwkv_ln_b,
           diff_lambda_q1, diff_lambda_k1, diff_lambda_q2, diff_lambda_k2, diff_subln,
           w_branch, w_out, ffn2_norm_pre, ffn2_norm_post, ffn2_w_gate, ffn2_w_up, ffn2_w_down):
    B, L, D = x.shape
    depth = w_in.shape[0]
    cfg = _tiles(B, L)
    x2 = x.reshape(B * L, D)
    for l in range(depth):
        x2 = _ffn(x2, ffn1_norm_pre[l].reshape(1, D), ffn1_norm_post[l].reshape(1, D),
                  *_ffn_weights(ffn1_w_gate[l], ffn1_w_up[l], ffn1_w_down[l], cfg["tf"]),
                  tm=cfg["tm_ffn"])
        rwkv = (rwkv_mu[l], rwkv_w0[l], rwkv_w_up[l], rwkv_a0[l], rwkv_a_up[l], rwkv_g_up[l],
                rwkv_k_k[l], rwkv_k_a[l], rwkv_r_k[l], rwkv_ln_w[l], rwkv_ln_b[l])
        diffp = (diff_lambda_q1[l], diff_lambda_k1[l], diff_lambda_q2[l], diff_lambda_k2[l],
                 diff_subln[l])
        x2 = _mixer(x2, B, L, l, cfg, mix_norm_pre[l], mix_norm_post[l], w_in[l], rwkv, diffp,
                    w_branch[l], w_out[l])
        x2 = _ffn(x2, ffn2_norm_pre[l].reshape(1, D), ffn2_norm_post[l].reshape(1, D),
                  *_ffn_weights(ffn2_w_gate[l], ffn2_w_up[l], ffn2_w_down[l], cfg["tf"]),
                  tm=cfg["tm_ffn"])
    return x2.reshape(B, L, D)
```
